```python
import math
import jax, jax.numpy as jnp
from jax import lax
import numpy as np

D_MODEL = 1024
BATCH = 8
SEQ = 4096
DEPTH = 4

N_MEM = 256
MIX_WIDTH = D_MODEL
N_GROUPS = 4
GROUP_WIDTH = MIX_WIDTH // N_GROUPS
CONF_WIDTH = 31
HG_HEADS = 4
HG_DK = GROUP_WIDTH // HG_HEADS
HG_DV = GROUP_WIDTH // HG_HEADS
HG_CHUNK = 64
SC_WIDTH = 3
MLA_HEADS = 4
MLA_NOPE = 64
MLA_ROPE = 32
MLA_V = GROUP_WIDTH // MLA_HEADS
MLA_Q_LORA = 256
MLA_KV_LORA = 128
ROPE_BASE = 10000.0
Q_BLOCK = 128
XA_HEADS = 4
XA_HEAD_DIM = 64
D_FF = -(-8 * D_MODEL // (3 * 256)) * 256
EPS = 1e-6
A_COLS = 2 * GROUP_WIDTH
B_COLS = 5 * GROUP_WIDTH
C_COLS = 3 * GROUP_WIDTH
D_COLS = MLA_Q_LORA + MLA_KV_LORA + MLA_ROPE
IN_COLS = A_COLS + B_COLS + C_COLS + D_COLS
SPLITS = [A_COLS, A_COLS + B_COLS, A_COLS + B_COLS + C_COLS]

kernel_name = "hybrid_parallel_group_encoder"


def rms_norm(x, g):
    xf = x.astype(jnp.float32)
    y = xf * lax.rsqrt(jnp.mean(xf * xf, axis=-1, keepdims=True) + EPS)
    return (y * g.astype(jnp.float32)).astype(x.dtype)


def layer_norm(x, g, b):
    xf = x.astype(jnp.float32)
    mu = jnp.mean(xf, axis=-1, keepdims=True)
    xc = xf - mu
    y = xc * lax.rsqrt(jnp.mean(xc * xc, axis=-1, keepdims=True) + EPS)
    return (y * g.astype(jnp.float32) + b.astype(jnp.float32)).astype(x.dtype)


def depthwise_conv(x, w, b):
    c = x.shape[-1]
    y = lax.conv_general_dilated(x, w[:, None, :].astype(x.dtype), window_strides=(1,),
                                 padding='SAME', dimension_numbers=('NWC', 'WIO', 'NWC'),
                                 feature_group_count=c)
    return y + b.astype(x.dtype)


def rope_tables(positions):
    inv = ROPE_BASE ** (-jnp.arange(0, MLA_ROPE, 2, dtype=jnp.float32) / MLA_ROPE)
    ang = positions.astype(jnp.float32)[..., None] * inv
    return jnp.cos(ang)[:, :, None, :], jnp.sin(ang)[:, :, None, :]


def apply_rope(x, cos, sin):
    x1, x2 = jnp.split(x, 2, axis=-1)
    return jnp.concatenate([x1 * cos - x2 * sin, x2 * cos + x1 * sin], axis=-1).astype(x.dtype)


def blocked_attention(q, k, v):
    b, s, h, d = q.shape
    scale = d ** -0.5
    qb = q.reshape(b, s // Q_BLOCK, Q_BLOCK, h, d).transpose(1, 0, 2, 3, 4)

    def one_block(qblk):
        sc = jnp.einsum('bqhd,bkhd->bhqk', qblk, k).astype(jnp.float32) * scale
        p = jax.nn.softmax(sc, axis=-1)
        return jnp.einsum('bhqk,bkhe->bqhe', p.astype(v.dtype), v)

    out = lax.map(one_block, qb)
    return out.transpose(1, 0, 2, 3, 4).reshape(b, s, h, v.shape[-1])


def conformer_conv(ua, dw_w, dw_b, ln_g, ln_b):
    a, gate = jnp.split(ua, 2, axis=-1)
    h = a * jax.nn.sigmoid(gate)
    h = depthwise_conv(h, dw_w, dw_b)
    h = layer_norm(h, ln_g, ln_b)
    return jax.nn.silu(h)


def hgrn2_bidir(ub, lb_fwd, lb_bwd, onorm_g):
    b, s, _ = ub.shape
    f32 = jnp.float32
    q, zf, zb, v, g = jnp.split(ub, 5, axis=-1)

    def log_forget(z, lb):
        zf32 = z.astype(f32)
        return jnp.log(jax.nn.sigmoid(zf32) + lb * jax.nn.sigmoid(-zf32))

    rev = lambda t: t[:, ::-1]
    logf = jnp.stack([log_forget(zf, lb_fwd), log_forget(rev(zb), lb_bwd)])
    qd = jnp.stack([q, rev(q)]).astype(f32)
    vd = jnp.stack([v, rev(v)]).astype(f32)
    n_chunks = s // HG_CHUNK

    def to_chunks(t, d):
        return t.reshape(2, b, n_chunks, HG_CHUNK, HG_HEADS, d).transpose(2, 0, 1, 4, 3, 5)

    mask = jnp.tril(jnp.ones((HG_CHUNK, HG_CHUNK), dtype=f32))[..., None]

    def step(state, inp):
        qc, lfc, vc = inp
        bcum = jnp.cumsum(lfc, axis=-2)
        kc = -jnp.expm1(lfc)
        o_inter = jnp.einsum('zbhcd,zbhde->zbhce', qc * jnp.exp(bcum), state)
        diff = bcum[..., :, None, :] - bcum[..., None, :, :]
        decay = jnp.exp(diff * mask) * mask
        attn = jnp.sum(qc[..., :, None, :] * kc[..., None, :, :] * decay, axis=-1)
        o_intra = jnp.einsum('zbhts,zbhse->zbhte', attn, vc)
        blast = bcum[..., -1:, :]
        new_state = jnp.exp(blast[..., 0, :])[..., :, None] * state + \
            jnp.einsum('zbhsd,zbhse->zbhde', kc * jnp.exp(blast - bcum), vc)
        return new_state, o_inter + o_intra

    init = jnp.zeros((2, b, HG_HEADS, HG_DK, HG_DV), f32)
    _, ys = lax.scan(step, init, (to_chunks(qd, HG_DK), to_chunks(logf, HG_DK), to_chunks(vd, HG_DV)))
    ys = ys.transpose(1, 2, 0, 4, 3, 5).reshape(2, b, s, HG_HEADS, HG_DV)
    o = ys[0] + ys[1][:, ::-1]
    o = rms_norm(o, onorm_g.reshape(HG_HEADS, HG_DV)) * \
        jax.nn.silu(g.astype(f32)).reshape(b, s, HG_HEADS, HG_DV)
    return o.reshape(b, s, GROUP_WIDTH).astype(ub.dtype)


def short_gated_conv(uc, dw_w, dw_b):
    gb, gc, xin = jnp.split(uc, 3, axis=-1)
    return gb * depthwise_conv(gc * xin, dw_w, dw_b)


def mla(ud, cos, sin, qa_g, wuq, kva_g, wukv, qn_g, kn_g):
    b, s, _ = ud.shape
    cq, ckv, kr = jnp.split(ud, [MLA_Q_LORA, MLA_Q_LORA + MLA_KV_LORA], axis=-1)
    q = (rms_norm(cq, qa_g) @ wuq).reshape(b, s, MLA_HEADS, MLA_NOPE + MLA_ROPE)
    kv = (rms_norm(ckv, kva_g) @ wukv).reshape(b, s, MLA_HEADS, MLA_NOPE + MLA_V)
    k_nope, v = jnp.split(kv, [MLA_NOPE], axis=-1)
    k = jnp.concatenate([k_nope, jnp.broadcast_to(kr[:, :, None, :], (b, s, MLA_HEADS, MLA_ROPE))], axis=-1)
    q = rms_norm(q, qn_g)
    k = rms_norm(k, kn_g)
    q = jnp.concatenate([q[..., :MLA_NOPE], apply_rope(q[..., MLA_NOPE:], cos, sin)], axis=-1)
    k = jnp.concatenate([k[..., :MLA_NOPE], apply_rope(k[..., MLA_NOPE:], cos, sin)], axis=-1)
    return blocked_attention(q, k, v).reshape(b, s, MLA_HEADS * MLA_V)


def setup_inputs(seed: int = 0) -> dict:
    key = jax.random.key(seed)
    ks = iter(jax.random.split(key, 40))
    L = DEPTH
    f32 = jnp.float32

    def nrm(shape, fan_in):
        return jax.random.normal(next(ks), shape, f32) * fan_in ** -0.5

    def gain(shape):
        return 1.0 + 0.01 * jax.random.normal(next(ks), shape, f32)

    def bias(shape):
        return 0.01 * jax.random.normal(next(ks), shape, f32)

    x = jax.random.normal(next(ks), (BATCH, SEQ, D_MODEL), f32)
    mem = jax.random.normal(next(ks), (BATCH, N_MEM, D_MODEL), f32)
    positions = jnp.arange(SEQ, dtype=jnp.int32)[None, :] + \
        jax.random.randint(next(ks), (BATCH, 1), 0, 1024, dtype=jnp.int32)
    return {
        "x": x,
        "mem": mem,
        "positions": positions,
        "g_mix": gain((L, D_MODEL)),
        "w_in": nrm((L, D_MODEL, IN_COLS), D_MODEL),
        "a_dw_w": nrm((L, CONF_WIDTH, GROUP_WIDTH), CONF_WIDTH),
        "a_dw_b": bias((L, GROUP_WIDTH)),
        "a_ln_g": gain((L, GROUP_WIDTH)),
        "a_ln_b": bias((L, GROUP_WIDTH)),
        "h_gamma": 0.5 * jax.random.normal(next(ks), (2, L, GROUP_WIDTH), f32),
        "h_onorm_g": gain((L, GROUP_WIDTH)),
        "c_dw_w": nrm((L, SC_WIDTH, GROUP_WIDTH), SC_WIDTH),
        "c_dw_b": bias((L, GROUP_WIDTH)),
        "m_qa_g": gain((L, MLA_Q_LORA)),
        "m_wuq": nrm((L, MLA_Q_LORA, MLA_HEADS * (MLA_NOPE + MLA_ROPE)), MLA_Q_LORA),
        "m_kva_g": gain((L, MLA_KV_LORA)),
        "m_wukv": nrm((L, MLA_KV_LORA, MLA_HEADS * (MLA_NOPE + MLA_V)), MLA_KV_LORA),
        "m_qn_g": gain((L, MLA_NOPE + MLA_ROPE)),
        "m_kn_g": gain((L, MLA_NOPE + MLA_ROPE)),
        "g_branch": gain((L, MIX_WIDTH)),
        "w_out": nrm((L, MIX_WIDTH, D_MODEL), MIX_WIDTH),
        "g_xq": gain((L, D_MODEL)),
        "g_mem": gain((L, D_MODEL)),
        "x_wq": nrm((L, D_MODEL, XA_HEADS * XA_HEAD_DIM), D_MODEL),
        "x_wkv": nrm((L, D_MODEL, 2 * XA_HEADS * XA_HEAD_DIM), D_MODEL),
        "x_qn_g": gain((L, XA_HEAD_DIM)),
        "x_kn_g": gain((L, XA_HEAD_DIM)),
        "x_wo": nrm((L, XA_HEADS * XA_HEAD_DIM, D_MODEL), XA_HEADS * XA_HEAD_DIM),
        "g_ffn": gain((L, D_MODEL)),
        "f_w13": nrm((L, D_MODEL, 2 * D_FF), D_MODEL),
        "f_w2": nrm((L, D_FF, D_MODEL), D_FF),
    }


def reference(x, mem, positions, g_mix, w_in, a_dw_w, a_dw_b, a_ln_g, a_ln_b, h_gamma, h_onorm_g,
              c_dw_w, c_dw_b, m_qa_g, m_wuq, m_kva_g, m_wukv, m_qn_g, m_kn_g, g_branch, w_out,
              g_xq, g_mem, x_wq, x_wkv, x_qn_g, x_kn_g, x_wo, g_ffn, f_w13, f_w2):
    b, s, _ = x.shape
    n_mem = mem.shape[1]
    cos, sin = rope_tables(positions)
    p = jax.nn.softmax(h_gamma.astype(jnp.float32), axis=1)
    lower = jnp.cumsum(p, axis=1) - p[:, :1]
    for l in range(DEPTH):
        n = rms_norm(x, g_mix[l])
        u = n @ w_in[l]
        ua, ub, uc, ud = jnp.split(u, SPLITS, axis=-1)
        y_a = conformer_conv(ua, a_dw_w[l], a_dw_b[l], a_ln_g[l], a_ln_b[l])
        y_b = hgrn2_bidir(ub, lower[0, l], lower[1, l], h_onorm_g[l])
        y_c = short_gated_conv(uc, c_dw_w[l], c_dw_b[l])
        y_d = mla(ud, cos, sin, m_qa_g[l], m_wuq[l], m_kva_g[l], m_wukv[l], m_qn_g[l], m_kn_g[l])
        cat = jnp.concatenate([y_a, y_b, y_c, y_d], axis=-1).reshape(b, s, N_GROUPS, GROUP_WIDTH)
        cat = rms_norm(cat, g_branch[l].reshape(N_GROUPS, GROUP_WIDTH)).reshape(b, s, MIX_WIDTH)
        x = x + cat @ w_out[l]
        hq = rms_norm(x, g_xq[l])
        mn = rms_norm(mem, g_mem[l])
        q = (hq @ x_wq[l]).reshape(b, s, XA_HEADS, XA_HEAD_DIM)
        kv = (mn @ x_wkv[l]).reshape(b, n_mem, XA_HEADS, 2 * XA_HEAD_DIM)
        k, v = jnp.split(kv, 2, axis=-1)
        q = rms_norm(q, x_qn_g[l])
        k = rms_norm(k, x_kn_g[l])
        o = blocked_attention(q, k, v).reshape(b, s, XA_HEADS * XA_HEAD_DIM)
        x = x + o @ x_wo[l]
        h = rms_norm(x, g_ffn[l])
        a1, a3 = jnp.split(h @ f_w13[l], 2, axis=-1)
        x = x + (jax.nn.silu(a1) * a3) @ f_w2[l]
    return x
```

```python
import functools

import jax
import jax.numpy as jnp
from jax import lax
from jax.experimental import pallas as pl
from jax.experimental.pallas import tpu as pltpu

F32 = jnp.float32
BF16 = jnp.bfloat16
EPS = 1e-6

GROUP = 256
HEADS = 4
HDIM = 64
ROPE = 32
QK = HDIM + ROPE
HPAD = 128
CONF_W = 31
SC_W = 3
HALO = 16
STEP = 16
ROPE_BASE = 10000.0
VMEM_LIMIT = 56 * 1024 * 1024


def _rms(x, g):
    return x * lax.rsqrt(jnp.mean(x * x, axis=-1, keepdims=True) + EPS) * g


def _sigmoid(x):
    return 1.0 / (1.0 + jnp.exp(-x))


def _dot(a, b):
    return jnp.dot(a, b, preferred_element_type=F32)


def _dot_nt(a, b):
    return lax.dot_general(a, b, (((1,), (1,)), ((), ())), preferred_element_type=F32)


def _dot_tn(a, b):
    return lax.dot_general(a, b, (((0,), (0,)), ((), ())), preferred_element_type=F32)


def _split_dot(m, x):
    hi = x.astype(BF16)
    lo = (x - hi.astype(F32)).astype(BF16)
    return _dot(m, hi) + _dot(m, lo)


def _head_ones(n, width):
    r = lax.broadcasted_iota(jnp.int32, (n, n), 0) // width
    c = lax.broadcasted_iota(jnp.int32, (n, n), 1) // width
    return r == c


def _head_rsqrt(x, ones_bd, width):
    return lax.rsqrt(_split_dot_rhs(x * x, ones_bd) * (1.0 / width) + EPS)


def _split_dot_rhs(x, m):
    hi = x.astype(BF16)
    lo = (x - hi.astype(F32)).astype(BF16)
    return _dot(hi, m) + _dot(lo, m)


def _params(*sem):
    return pltpu.CompilerParams(dimension_semantics=sem, vmem_limit_bytes=VMEM_LIMIT)


def _full(shape):
    nd = len(shape)
    return pl.BlockSpec(shape, lambda *_: (0,) * nd)


def _rope_kernel(pos_ref, inv_ref, ct_ref, sg_ref):
    ang = pos_ref[...] * inv_ref[...]
    lane = lax.broadcasted_iota(jnp.int32, ang.shape, 1)
    c = jnp.cos(ang)
    s = jnp.sin(ang)
    half = ROPE // 2
    ct_ref[...] = jnp.where(lane < HDIM, 1.0, jnp.where(lane < QK, c, 0.0))
    sg_ref[...] = jnp.where(lane < HDIM, 0.0,
                            jnp.where(lane < HDIM + half, -s, jnp.where(lane < QK, s, 0.0)))


def _rope_tables(pos_f, inv_row, tm):
    t = pos_f.shape[0]
    return pl.pallas_call(
        _rope_kernel,
        grid=(t // tm,),
        in_specs=[pl.BlockSpec((tm, 1), lambda i: (i, 0)), _full((1, HPAD))],
        out_specs=[pl.BlockSpec((tm, HPAD), lambda i: (i, 0))] * 2,
        out_shape=[jax.ShapeDtypeStruct((t, HPAD), F32)] * 2,
        compiler_params=_params("parallel"),
        name="rope_tables",
    )(pos_f, inv_row)


def _memkv_kernel(mem_ref, g_ref, w_ref, kn_ref, k_ref, v_ref):
    mn = _rms(mem_ref[0], g_ref[0]).astype(BF16)
    kv = _dot(mn, w_ref[0])
    k = kv[:, :GROUP]
    ones_bd = jnp.where(_head_ones(GROUP, HDIM), 1.0, 0.0).astype(BF16)
    k_ref[0, 0] = (k * _head_rsqrt(k, ones_bd, HDIM) * kn_ref[0]).astype(BF16)
    v_ref[0, 0] = kv[:, GROUP:].astype(BF16)


def _mem_kv(mem, g_mem, wkv, kn_g):
    b, n, d = mem.shape
    nl = wkv.shape[0]
    return pl.pallas_call(
        _memkv_kernel,
        grid=(nl, b),
        in_specs=[pl.BlockSpec((1, n, d), lambda l, i: (i, 0, 0)),
                  pl.BlockSpec((1, 1, d), lambda l, i: (l, 0, 0)),
                  pl.BlockSpec((1, d, 2 * GROUP), lambda l, i: (l, 0, 0)),
                  pl.BlockSpec((1, 1, GROUP), lambda l, i: (l, 0, 0))],
        out_specs=[pl.BlockSpec((1, 1, n, GROUP), lambda l, i: (l, i, 0, 0))] * 2,
        out_shape=[jax.ShapeDtypeStruct((nl, b, n, GROUP), BF16)] * 2,
        compiler_params=_params("parallel", "parallel"),
        name="mem_kv",
    )(mem, g_mem, wkv, kn_g)


def _inproj_kernel(x_ref, g_ref, w_ref, ua_ref, ub_ref, uc_ref, ud_ref):
    n = _rms(x_ref[...], g_ref[...]).astype(BF16)
    c = 0
    for ref in (ua_ref, ub_ref, uc_ref, ud_ref):
        w = ref.shape[1]
        ref[...] = _dot(n, w_ref[:, c:c + w])
        c += w


def _in_proj(x2, g, w, widths, tm):
    t, d = x2.shape
    return pl.pallas_call(
        _inproj_kernel,
        grid=(t // tm,),
        in_specs=[pl.BlockSpec((tm, d), lambda i: (i, 0)), _full((1, d)), _full(w.shape)],
        out_specs=[pl.BlockSpec((tm, n), lambda i: (i, 0)) for n in widths],
        out_shape=[jax.ShapeDtypeStruct((t, n), F32) for n in widths],
        compiler_params=_params("parallel"),
        name="in_proj",
    )(x2, g, w)


def _conv_kernel(ua_ref, uap_ref, uan_ref, uc_ref, ucp_ref, ucn_ref,
                 aw_ref, ab_ref, lng_ref, lnb_ref, cw_ref, cb_ref, gba_ref, gbc_ref,
                 ya_ref, yc_ref, ha_scr, hc_scr, *, sub):
    i = pl.program_id(1)
    n = pl.num_programs(1)
    tc = ua_ref.shape[1]
    has_prev = jnp.where(i > 0, 1.0, 0.0)
    has_next = jnp.where(i < n - 1, 1.0, 0.0)

    def glu(u):
        return u[:, :GROUP] * _sigmoid(u[:, GROUP:])

    def gated(u):
        return u[:, GROUP:2 * GROUP] * u[:, 2 * GROUP:]

    ha_scr[0:HALO, :] = glu(uap_ref[0]) * has_prev
    ha_scr[HALO:HALO + tc, :] = glu(ua_ref[0])
    ha_scr[HALO + tc:, :] = glu(uan_ref[0]) * has_next
    hc_scr[0:HALO, :] = gated(ucp_ref[0]) * has_prev
    hc_scr[HALO:HALO + tc, :] = gated(uc_ref[0])
    hc_scr[HALO + tc:, :] = gated(ucn_ref[0]) * has_next

    pad_a = (CONF_W - 1) // 2
    pad_c = (SC_W - 1) // 2
    for r0 in range(0, tc, sub):
        acc = jnp.broadcast_to(ab_ref[...], (sub, GROUP))
        for j in range(CONF_W):
            acc = acc + aw_ref[j:j + 1, :] * ha_scr[r0 + HALO - pad_a + j:r0 + HALO - pad_a + j + sub, :]
        mu = jnp.mean(acc, axis=-1, keepdims=True)
        xc = acc - mu
        y = xc * lax.rsqrt(jnp.mean(xc * xc, axis=-1, keepdims=True) + EPS) * lng_ref[...] + lnb_ref[...]
        y = y * _sigmoid(y)
        ya_ref[0, r0:r0 + sub, :] = _rms(y, gba_ref[...]).astype(BF16)

        acc = jnp.broadcast_to(cb_ref[...], (sub, GROUP))
        for j in range(SC_W):
            acc = acc + cw_ref[j:j + 1, :] * hc_scr[r0 + HALO - pad_c + j:r0 + HALO - pad_c + j + sub, :]
        y = uc_ref[0, r0:r0 + sub, 0:GROUP] * acc
        yc_ref[0, r0:r0 + sub, :] = _rms(y, gbc_ref[...]).astype(BF16)


def _conv_mixers(ua, uc, aw, ab, lng, lnb, cw, cb, gba, gbc, tc):
    b, s, _ = ua.shape
    nb = tc // HALO
    last = s // HALO - 1
    main = lambda w: pl.BlockSpec((1, tc, w), lambda bi, i: (bi, i, 0))
    prev = lambda w: pl.BlockSpec((1, HALO, w), lambda bi, i: (bi, jnp.maximum(i * nb - 1, 0), 0))
    nxt = lambda w: pl.BlockSpec((1, HALO, w), lambda bi, i: (bi, jnp.minimum((i + 1) * nb, last), 0))
    wa, wc = ua.shape[2], uc.shape[2]
    return pl.pallas_call(
        functools.partial(_conv_kernel, sub=64),
        grid=(b, s // tc),
        in_specs=[main(wa), prev(wa), nxt(wa), main(wc), prev(wc), nxt(wc),
                  _full(aw.shape), _full(ab.shape), _full(lng.shape), _full(lnb.shape),
                  _full(cw.shape), _full(cb.shape), _full(gba.shape), _full(gbc.shape)],
        out_specs=[pl.BlockSpec((1, tc, GROUP), lambda bi, i: (bi, i, 0))] * 2,
        out_shape=[jax.ShapeDtypeStruct((b, s, GROUP), BF16)] * 2,
        scratch_shapes=[pltpu.VMEM((tc + 2 * HALO, GROUP), F32)] * 2,
        compiler_params=_params("parallel", "parallel"),
        name="conv_mixers",
    )(ua, ua, ua, uc, uc, uc, aw, ab, lng, lnb, cw, cb, gba, gbc)


def _hgrn_kernel(gam_ref, qf_ref, zf_ref, vf_ref, qb_ref, zb_ref, vb_ref, of_ref, ob_ref,
                 st_f, st_b, kk_f, kk_b, b_f, b_b, lam_f, lam_b, qe_f, qe_b, kt_f, kt_b,
                 ones_scr, mask_scr, *, layer):
    i = pl.program_id(1)
    tb = qf_ref.shape[1]
    nst = tb // STEP
    cs = 256

    @pl.when(i == 0)
    def _():
        st_f[...] = jnp.zeros_like(st_f)
        st_b[...] = jnp.zeros_like(st_b)

    bd = _head_ones(GROUP, HDIM)
    ones_scr[...] = jnp.where(bd, 1.0, 0.0).astype(BF16)
    mask_scr[...] = jnp.where(bd, 1.0, 0.0)

    ti = lax.broadcasted_iota(jnp.int32, (cs, cs), 0)
    si = lax.broadcasted_iota(jnp.int32, (cs, cs), 1)
    same = (ti // STEP) == (si // STEP)
    m_all = jnp.where(same, 1.0, 0.0).astype(BF16)

    def prep(d, q_ref, z_ref, kk_scr, b_scr, lam_scr, qe_scr, kt_scr):
        g = gam_ref[d]
        e = jnp.exp(g - jnp.max(g, axis=0, keepdims=True))
        tot = jnp.sum(e, axis=0, keepdims=True)
        if layer == 0:
            lb = jnp.zeros_like(tot)
        else:
            lb = jnp.sum(e[1:layer + 1], axis=0, keepdims=True) / tot
        tri = (si <= ti) if d == 0 else (si >= ti)
        m_tri = jnp.where(same & tri, 1.0, 0.0).astype(BF16)
        for r0 in range(0, tb, cs):
            z = z_ref[0, r0:r0 + cs, :]
            f = lb + (1.0 - lb) * _sigmoid(z)
            kk = (1.0 - lb) * _sigmoid(-z)
            lf = jnp.log(f)
            bloc = _split_dot(m_tri, lf)
            btot = _split_dot(m_all, lf)
            kk_scr[r0:r0 + cs, :] = kk
            b_scr[r0:r0 + cs, :] = bloc
            lam_scr[r0:r0 + cs, :] = jnp.exp(btot)
            qe_scr[r0:r0 + cs, :] = (q_ref[0, r0:r0 + cs, :] * jnp.exp(bloc)).astype(BF16)
            kt_scr[r0:r0 + cs, :] = (kk * jnp.exp(btot - bloc)).astype(BF16)

    prep(0, qf_ref, zf_ref, kk_f, b_f, lam_f, qe_f, kt_f)
    prep(1, qb_ref, zb_ref, kk_b, b_b, lam_b, qe_b, kt_b)

    trow = lax.broadcasted_iota(jnp.int32, (STEP, GROUP), 0)

    def one_dir(r, rev, q_ref, v_ref, o_ref, st, kk_scr, b_scr, lam_scr, qe_scr, kt_scr):
        rows = pl.ds(r, STEP)
        q = q_ref[0, rows, :]
        b = b_scr[rows, :]
        state = st[...]
        o = _dot_nt(qe_scr[rows, :], state.astype(BF16))
        parts = []
        for s in range(STEP):
            brow = b_scr[pl.ds(r + s, 1), :]
            krow = kk_scr[pl.ds(r + s, 1), :]
            dec = q * krow * jnp.exp(b - brow)
            live = (trow <= s) if rev else (trow >= s)
            parts.append(jnp.where(live, dec, 0.0).astype(BF16))
        att = _dot(jnp.concatenate(parts, axis=0), ones_scr[...])
        for s in range(STEP):
            o = o + att[s * STEP:(s + 1) * STEP, :] * v_ref[0, pl.ds(r + s, 1), :]
        o_ref[0, rows, :] = o
        upd = _dot_tn(v_ref[0, rows, :].astype(BF16), kt_scr[rows, :])
        st[...] = lam_scr[pl.ds(r, 1), :] * state + upd * mask_scr[...]

    def body(k, carry):
        rf = pl.multiple_of(k * STEP, STEP)
        rb = pl.multiple_of((nst - 1 - k) * STEP, STEP)
        one_dir(rf, False, qf_ref, vf_ref, of_ref, st_f, kk_f, b_f, lam_f, qe_f, kt_f)
        one_dir(rb, True, qb_ref, vb_ref, ob_ref, st_b, kk_b, b_b, lam_b, qe_b, kt_b)
        return carry

    lax.fori_loop(0, nst, body, 0)


def _hgrn(ub, gamma, layer, tb):
    b, s, _ = ub.shape
    nb = s // tb
    col = lambda j, rev: pl.BlockSpec(
        (1, tb, GROUP), (lambda bi, i: (bi, nb - 1 - i, j)) if rev else (lambda bi, i: (bi, i, j)))
    vm = lambda dt: pltpu.VMEM((tb, GROUP), dt)
    return pl.pallas_call(
        functools.partial(_hgrn_kernel, layer=layer),
        grid=(b, nb),
        in_specs=[_full(gamma.shape),
                  col(0, False), col(1, False), col(3, False),
                  col(0, True), col(2, True), col(3, True)],
        out_specs=[pl.BlockSpec((1, tb, GROUP), lambda bi, i: (bi, i, 0)),
                   pl.BlockSpec((1, tb, GROUP), lambda bi, i: (bi, nb - 1 - i, 0))],
        out_shape=[jax.ShapeDtypeStruct((b, s, GROUP), F32)] * 2,
        scratch_shapes=[pltpu.VMEM((GROUP, GROUP), F32)] * 2
        + [vm(F32)] * 6 + [vm(BF16)] * 4
        + [pltpu.VMEM((GROUP, GROUP), BF16), pltpu.VMEM((GROUP, GROUP), F32)],
        compiler_params=_params("parallel", "arbitrary"),
        name="hgrn2",
    )(gamma, ub, ub, ub, ub, ub, ub)


def _mla_prep_kernel(ud_ref, ct_ref, sg_ref, qag_ref, kvag_ref, wq_ref, wk_ref,
                     gq_ref, gqs_ref, gk_ref, gks_ref, q_ref, k_ref, v_ref):
    ud = ud_ref[...]
    nw = HEADS * HPAD
    nq = _rms(ud[:, :GROUP], qag_ref[...]).astype(BF16)
    qq = _dot(nq, wq_ref[...])
    nkv = _rms(ud[:, GROUP:GROUP + HPAD], kvag_ref[...]).astype(BF16)
    kr = ud[:, GROUP + HPAD:]
    kr_hi = kr.astype(BF16)
    kr_lo = (kr - kr_hi.astype(F32)).astype(BF16)
    kk = _dot(jnp.concatenate([nkv, kr_hi, kr_lo], axis=-1), wk_ref[...])
    ct = ct_ref[...]
    sg = sg_ref[...]
    scale = QK ** -0.5
    for h in range(HEADS):
        sl = slice(h * HPAD, (h + 1) * HPAD)
        sw = slice(nw + h * HPAD, nw + (h + 1) * HPAD)
        x = qq[:, sl]
        r = lax.rsqrt(jnp.sum(x * x, axis=-1, keepdims=True) * (1.0 / QK) + EPS) * scale
        q_ref[:, sl] = (r * (x * (gq_ref[...] * ct) + qq[:, sw] * (gqs_ref[...] * sg))).astype(BF16)
        x = kk[:, sl]
        r = lax.rsqrt(jnp.sum(x * x, axis=-1, keepdims=True) * (1.0 / QK) + EPS)
        k_ref[:, sl] = (r * (x * (gk_ref[...] * ct) + kk[:, sw] * (gks_ref[...] * sg))).astype(BF16)
    v_ref[...] = kk[:, 2 * nw:].astype(BF16)


def _mla_prep(ud, ct, sg, qag, kvag, wq, wk, gq, gqs, gk, gks, tm):
    t, w = ud.shape
    nw = HEADS * HPAD
    row = lambda n: pl.BlockSpec((tm, n), lambda i: (i, 0))
    return pl.pallas_call(
        _mla_prep_kernel,
        grid=(t // tm,),
        in_specs=[row(w), row(HPAD), row(HPAD)] + [_full(a.shape) for a in (qag, kvag, wq, wk, gq, gqs, gk, gks)],
        out_specs=[row(nw), row(nw), row(GROUP)],
        out_shape=[jax.ShapeDtypeStruct((t, nw), BF16), jax.ShapeDtypeStruct((t, nw), BF16),
                   jax.ShapeDtypeStruct((t, GROUP), BF16)],
        compiler_params=_params("parallel"),
        name="mla_prep",
    )(ud, ct, sg, qag, kvag, wq, wk, gq, gqs, gk, gks)


def _mla_attn_kernel(q_ref, k_ref, v_ref, gb_ref, o_ref):
    tq = q_ref.shape[1]
    lane = lax.broadcasted_iota(jnp.int32, (tq, GROUP), 1) // HDIM
    acc = jnp.zeros((tq, GROUP), F32)
    for h in range(HEADS):
        sl = slice(h * HPAD, (h + 1) * HPAD)
        s = _dot_nt(q_ref[0, :, sl], k_ref[0, :, sl])
        p = jnp.exp(s - jnp.max(s, axis=-1, keepdims=True))
        l = jnp.sum(p, axis=-1, keepdims=True)
        pv = _dot(p.astype(BF16), v_ref[0])
        acc = jnp.where(lane == h, pv / l, acc)
    o_ref[0] = _rms(acc, gb_ref[...]).astype(BF16)


def _mla_attn(q, k, v, gb, tq):
    b, s, nw = q.shape
    return pl.pallas_call(
        _mla_attn_kernel,
        grid=(b, s // tq),
        in_specs=[pl.BlockSpec((1, tq, nw), lambda bi, i: (bi, i, 0)),
                  pl.BlockSpec((1, s, nw), lambda bi, i: (bi, 0, 0)),
                  pl.BlockSpec((1, s, GROUP), lambda bi, i: (bi, 0, 0)),
                  _full(gb.shape)],
        out_specs=pl.BlockSpec((1, tq, GROUP), lambda bi, i: (bi, i, 0)),
        out_shape=jax.ShapeDtypeStruct((b, s, GROUP), BF16),
        compiler_params=_params("parallel", "arbitrary"),
        name="mla_attn",
    )(q, k, v, gb)


def _mix_out_kernel(x_ref, ya_ref, of_ref, ob_ref, g_ref, yc_ref, yd_ref, on_ref, gbb_ref, wo_ref,
                    gxq_ref, wq_ref, qn_ref, kx_ref, vx_ref, xwo_ref, o_ref):
    tm = x_ref.shape[1]
    ones_bd = jnp.where(_head_ones(GROUP, HDIM), 1.0, 0.0).astype(BF16)
    o = of_ref[0] + ob_ref[0]
    g = g_ref[0]
    yb = o * _head_rsqrt(o, ones_bd, HDIM) * on_ref[...] * (g * _sigmoid(g))
    yb = _rms(yb, gbb_ref[...]).astype(BF16)
    x1 = x_ref[0]
    for j, y in enumerate((ya_ref[0], yb, yc_ref[0], yd_ref[0])):
        x1 = x1 + _dot(y, wo_ref[j * GROUP:(j + 1) * GROUP, :])

    q = _dot(_rms(x1, gxq_ref[...]).astype(BF16), wq_ref[...])
    qn = q * _head_rsqrt(q, ones_bd, HDIM) * (qn_ref[...] * HDIM ** -0.5)
    lane = lax.broadcasted_iota(jnp.int32, (tm, GROUP), 1) // HDIM
    acc = jnp.zeros((tm, GROUP), F32)
    for h in range(HEADS):
        s = _dot_nt(jnp.where(lane == h, qn, 0.0).astype(BF16), kx_ref[0, 0])
        p = jnp.exp(s - jnp.max(s, axis=-1, keepdims=True))
        l = jnp.sum(p, axis=-1, keepdims=True)
        pv = _dot(p.astype(BF16), vx_ref[0, 0])
        acc = jnp.where(lane == h, pv / l, acc)
    o_ref[0] = x1 + _dot(acc.astype(BF16), xwo_ref[...])


def _mix_out(x, ya, of, ob, ub, yc, yd, on, gbb, wo, gxq, wq, qn, kx, vx, xwo, layer, tm):
    b, s, d = x.shape
    n = kx.shape[2]
    row = lambda w, j=0: pl.BlockSpec((1, tm, w), lambda bi, i: (bi, i, j))
    return pl.pallas_call(
        _mix_out_kernel,
        grid=(b, s // tm),
        in_specs=[row(d), row(GROUP), row(GROUP), row(GROUP), row(GROUP, 4), row(GROUP), row(GROUP),
                  _full(on.shape), _full(gbb.shape), _full(wo.shape), _full(gxq.shape), _full(wq.shape),
                  _full(qn.shape),
                  pl.BlockSpec((1, 1, n, GROUP), lambda bi, i: (layer, bi, 0, 0)),
                  pl.BlockSpec((1, 1, n, GROUP), lambda bi, i: (layer, bi, 0, 0)),
                  _full(xwo.shape)],
        out_specs=row(d),
        out_shape=jax.ShapeDtypeStruct((b, s, d), F32),
        compiler_params=_params("parallel", "parallel"),
        name="mix_out",
    )(x, ya, of, ob, ub, yc, yd, on, gbb, wo, gxq, wq, qn, kx, vx, xwo)


def _ffn_kernel(x_ref, g_ref, w13_ref, w2_ref, o_ref):
    x = x_ref[...]
    dff = w2_ref.shape[0]
    a = _dot(_rms(x, g_ref[...]).astype(BF16), w13_ref[...])
    a1 = a[:, :dff]
    o_ref[...] = x + _dot((a1 * _sigmoid(a1) * a[:, dff:]).astype(BF16), w2_ref[...])


def _ffn(x2, g, w13, w2, tm):
    t, d = x2.shape
    single = dict(pipeline_mode=pl.Buffered(1))
    return pl.pallas_call(
        _ffn_kernel,
        grid=(t // tm,),
        in_specs=[pl.BlockSpec((tm, d), lambda i: (i, 0)), _full((1, d)),
                  pl.BlockSpec(w13.shape, lambda i: (0, 0), **single),
                  pl.BlockSpec(w2.shape, lambda i: (0, 0), **single)],
        out_specs=pl.BlockSpec((tm, d), lambda i: (i, 0)),
        out_shape=jax.ShapeDtypeStruct((t, d), F32),
        compiler_params=_params("parallel"),
        name="ffn",
    )(x2, g, w13, w2)


def _pad_heads(w, width):
    lead = w.shape[:-1]
    w = w.reshape(lead + (HEADS, width))
    w = jnp.pad(w, [(0, 0)] * len(lead) + [(0, 0), (0, HPAD - width)])
    return w.reshape(lead + (HEADS * HPAD,))


def _swap_rope(w):
    half = ROPE // 2
    z = jnp.zeros_like(w[..., :HDIM])
    return jnp.concatenate([z, w[..., HDIM + half:], w[..., HDIM:HDIM + half]], axis=-1)


def _mla_weights(wuq, wukv, qn_g, kn_g):
    dq = wuq.shape[0]
    dkv = wukv.shape[0]
    wq3 = wuq.reshape(dq, HEADS, QK)
    wq = jnp.concatenate([_pad_heads(wuq, QK), _pad_heads(_swap_rope(wq3).reshape(dq, -1), QK)], axis=1)
    kv3 = wukv.reshape(dkv, HEADS, 2 * HDIM)
    k_nope = jnp.pad(kv3[..., :HDIM], [(0, 0), (0, 0), (0, HPAD - HDIM)]).reshape(dkv, -1)
    v_cols = kv3[..., HDIM:].reshape(dkv, -1)
    top = jnp.concatenate([k_nope, jnp.zeros_like(k_nope), v_cols], axis=1)
    eye = jnp.eye(ROPE, dtype=F32)
    place = jnp.pad(eye, [(0, 0), (HDIM, HPAD - QK)])
    place_sw = jnp.pad(jnp.roll(eye, ROPE // 2, axis=1), [(0, 0), (HDIM, HPAD - QK)])
    rope_rows = jnp.concatenate([jnp.tile(place, (1, HEADS)), jnp.tile(place_sw, (1, HEADS)),
                                 jnp.zeros((ROPE, v_cols.shape[1]), F32)], axis=1)
    rope_rows = jnp.pad(rope_rows, [(0, HPAD - ROPE), (0, 0)])
    wk = jnp.concatenate([top, rope_rows, rope_rows], axis=0)
    pad1 = lambda g: jnp.pad(g, (0, HPAD - QK))[None, :]
    return (wq.astype(BF16), wk.astype(BF16),
            pad1(qn_g), pad1(_swap_rope(qn_g)), pad1(kn_g), pad1(_swap_rope(kn_g)))


def kernel(x, mem, positions, g_mix, w_in, a_dw_w, a_dw_b, a_ln_g, a_ln_b, h_gamma, h_onorm_g, c_dw_w, c_dw_b, m_qa_g, m_wuq, m_kva_g, m_wukv, m_qn_g, m_kn_g, g_branch, w_out, g_xq, g_mem, x_wq, x_wkv, x_qn_g, x_kn_g, x_wo, g_ffn, f_w13, f_w2):
    b, s, d = x.shape
    t = b * s
    depth = w_in.shape[0]
    row = lambda v: v[None, :]
    tile4 = lambda v: jnp.tile(v, HEADS)[None, :]

    half = ROPE // 2
    inv = ROPE_BASE ** (-jnp.arange(0, ROPE, 2, dtype=F32) / ROPE)
    inv_row = jnp.concatenate([jnp.zeros((HDIM,), F32), inv, inv, jnp.zeros((HPAD - QK,), F32)])[None, :]
    ct, sg = _rope_tables(positions.astype(F32).reshape(t, 1), inv_row, 512)

    n_mem = mem.shape[1]
    kv4 = x_wkv.reshape(depth, d, HEADS, 2, HDIM).transpose(0, 1, 3, 2, 4).reshape(depth, d, 2 * GROUP)
    kx, vx = _mem_kv(mem, g_mem[:, None, :], kv4.astype(BF16), jnp.tile(x_kn_g, (1, HEADS))[:, None, :])

    widths = (2 * GROUP, 5 * GROUP, 3 * GROUP, 2 * GROUP)
    n_in = w_in.shape[2]
    for l in range(depth):
        w_in_l = jnp.pad(w_in[l], [(0, 0), (0, sum(widths) - n_in)]).astype(BF16)
        ua, ub, uc, ud = _in_proj(x.reshape(t, d), row(g_mix[l]), w_in_l, widths, 512)
        ua = ua.reshape(b, s, -1)
        ub = ub.reshape(b, s, -1)
        uc = uc.reshape(b, s, -1)
        gb = g_branch[l]
        ya, yc = _conv_mixers(ua, uc, a_dw_w[l], row(a_dw_b[l]), row(a_ln_g[l]), row(a_ln_b[l]),
                              c_dw_w[l], row(c_dw_b[l]), row(gb[:GROUP]), row(gb[2 * GROUP:3 * GROUP]), 512)
        of, ob = _hgrn(ub, h_gamma, l, 512)
        wq, wk, gq, gqs, gk, gks = _mla_weights(m_wuq[l], m_wukv[l], m_qn_g[l], m_kn_g[l])
        qr, kr, vr = _mla_prep(ud, ct, sg, row(m_qa_g[l]), row(m_kva_g[l]), wq, wk, gq, gqs, gk, gks, 512)
        yd = _mla_attn(qr.reshape(b, s, -1), kr.reshape(b, s, -1), vr.reshape(b, s, -1),
                       row(gb[3 * GROUP:]), 256)
        x = _mix_out(x, ya, of, ob, ub, yc, yd, row(h_onorm_g[l]), row(gb[GROUP:2 * GROUP]),
                     w_out[l].astype(BF16), row(g_xq[l]), x_wq[l].astype(BF16), tile4(x_qn_g[l]),
                     kx, vx, x_wo[l].astype(BF16), l, 256)
        x = _ffn(x.reshape(t, d), row(g_ffn[l]), f_w13[l].astype(BF16), f_w2[l].astype(BF16), 256).reshape(b, s, d)
    return x
```

```python
import functools

import jax
import jax.numpy as jnp
from jax import lax
from jax.experimental import pallas as pl
from jax.experimental.pallas import tpu as pltpu

F32 = jnp.float32
BF16 = jnp.bfloat16
EPS = 1e-6

GROUP = 256
HEADS = 4
HDIM = 64
ROPE = 32
QK = HDIM + ROPE
HPAD = 128
CONF_W = 31
SC_W = 3
HALO = 16
STEP = 16
ROPE_BASE = 10000.0
LOG2E = 1.4426950408889634
VMEM_LIMIT = 56 * 1024 * 1024


def _rms(x, g):
    return x * lax.rsqrt(jnp.mean(x * x, axis=-1, keepdims=True) + EPS) * g


def _sigmoid(x):
    return 1.0 / (1.0 + jnp.exp(-x))


def _dot(a, b):
    return jnp.dot(a, b, preferred_element_type=F32)


def _dot_nt(a, b):
    return lax.dot_general(a, b, (((1,), (1,)), ((), ())), preferred_element_type=F32)


def _dot_tn(a, b):
    return lax.dot_general(a, b, (((0,), (0,)), ((), ())), preferred_element_type=F32)


def _split_dot(m, x):
    hi = x.astype(BF16)
    lo = (x - hi.astype(F32)).astype(BF16)
    return _dot(m, hi) + _dot(m, lo)


def _head_ones(n, width):
    r = lax.broadcasted_iota(jnp.int32, (n, n), 0) // width
    c = lax.broadcasted_iota(jnp.int32, (n, n), 1) // width
    return r == c


def _head_rsqrt(x, ones_bd, width):
    return lax.rsqrt(_split_dot_rhs(x * x, ones_bd) * (1.0 / width) + EPS)


def _split_dot_rhs(x, m):
    hi = x.astype(BF16)
    lo = (x - hi.astype(F32)).astype(BF16)
    return _dot(hi, m) + _dot(lo, m)


def _params(*sem):
    return pltpu.CompilerParams(dimension_semantics=sem, vmem_limit_bytes=VMEM_LIMIT)


def _full(shape):
    nd = len(shape)
    return pl.BlockSpec(shape, lambda *_: (0,) * nd)


def _rope_kernel(pos_ref, inv_ref, ct_ref, sg_ref):
    ang = pos_ref[...] * inv_ref[...]
    lane = lax.broadcasted_iota(jnp.int32, ang.shape, 1)
    c = jnp.cos(ang)
    s = jnp.sin(ang)
    half = ROPE // 2
    ct_ref[...] = jnp.where(lane < HDIM, 1.0, jnp.where(lane < QK, c, 0.0))
    sg_ref[...] = jnp.where(lane < HDIM, 0.0,
                            jnp.where(lane < HDIM + half, -s, jnp.where(lane < QK, s, 0.0)))


def _rope_tables(pos_f, inv_row, tm):
    t = pos_f.shape[0]
    return pl.pallas_call(
        _rope_kernel,
        grid=(t // tm,),
        in_specs=[pl.BlockSpec((tm, 1), lambda i: (i, 0)), _full((1, HPAD))],
        out_specs=[pl.BlockSpec((tm, HPAD), lambda i: (i, 0))] * 2,
        out_shape=[jax.ShapeDtypeStruct((t, HPAD), F32)] * 2,
        compiler_params=_params("parallel"),
        name="rope_tables",
    )(pos_f, inv_row)


def _memkv_kernel(mem_ref, g_ref, w_ref, kn_ref, k_ref, v_ref):
    mn = _rms(mem_ref[0], g_ref[0]).astype(BF16)
    kv = _dot(mn, w_ref[0])
    k = kv[:, :GROUP]
    ones_bd = jnp.where(_head_ones(GROUP, HDIM), 1.0, 0.0).astype(BF16)
    k_ref[0, 0] = (k * _head_rsqrt(k, ones_bd, HDIM) * kn_ref[0]).astype(BF16)
    v_ref[0, 0] = kv[:, GROUP:].astype(BF16)


def _mem_kv(mem, g_mem, wkv, kn_g):
    b, n, d = mem.shape
    nl = wkv.shape[0]
    return pl.pallas_call(
        _memkv_kernel,
        grid=(nl, b),
        in_specs=[pl.BlockSpec((1, n, d), lambda l, i: (i, 0, 0)),
                  pl.BlockSpec((1, 1, d), lambda l, i: (l, 0, 0)),
                  pl.BlockSpec((1, d, 2 * GROUP), lambda l, i: (l, 0, 0)),
                  pl.BlockSpec((1, 1, GROUP), lambda l, i: (l, 0, 0))],
        out_specs=[pl.BlockSpec((1, 1, n, GROUP), lambda l, i: (l, i, 0, 0))] * 2,
        out_shape=[jax.ShapeDtypeStruct((nl, b, n, GROUP), BF16)] * 2,
        compiler_params=_params("parallel", "parallel"),
        name="mem_kv",
    )(mem, g_mem, wkv, kn_g)


def _inproj_kernel(x_ref, g_ref, w_ref, ua_ref, ub_ref, uc_ref, ud_ref):
    n = _rms(x_ref[...], g_ref[...]).astype(BF16)
    c = 0
    for ref in (ua_ref, ub_ref, uc_ref, ud_ref):
        w = ref.shape[1]
        ref[...] = _dot(n, w_ref[:, c:c + w])
        c += w


def _in_proj(x2, g, w, widths, tm):
    t, d = x2.shape
    return pl.pallas_call(
        _inproj_kernel,
        grid=(t // tm,),
        in_specs=[pl.BlockSpec((tm, d), lambda i: (i, 0)), _full((1, d)), _full(w.shape)],
        out_specs=[pl.BlockSpec((tm, n), lambda i: (i, 0)) for n in widths],
        out_shape=[jax.ShapeDtypeStruct((t, n), F32) for n in widths],
        compiler_params=_params("parallel"),
        name="in_proj",
    )(x2, g, w)


def _conv_kernel(ua_ref, uap_ref, uan_ref, uc_ref, ucp_ref, ucn_ref,
                 aw_ref, ab_ref, lng_ref, lnb_ref, cw_ref, cb_ref, gba_ref, gbc_ref,
                 ya_ref, yc_ref, ha_scr, hc_scr, *, sub):
    i = pl.program_id(1)
    n = pl.num_programs(1)
    tc = ua_ref.shape[1]
    has_prev = jnp.where(i > 0, 1.0, 0.0)
    has_next = jnp.where(i < n - 1, 1.0, 0.0)

    def glu(u):
        return u[:, :GROUP] * _sigmoid(u[:, GROUP:])

    def gated(u):
        return u[:, GROUP:2 * GROUP] * u[:, 2 * GROUP:]

    ha_scr[0:HALO, :] = glu(uap_ref[0]) * has_prev
    ha_scr[HALO:HALO + tc, :] = glu(ua_ref[0])
    ha_scr[HALO + tc:, :] = glu(uan_ref[0]) * has_next
    hc_scr[0:HALO, :] = gated(ucp_ref[0]) * has_prev
    hc_scr[HALO:HALO + tc, :] = gated(uc_ref[0])
    hc_scr[HALO + tc:, :] = gated(ucn_ref[0]) * has_next

    pad_a = (CONF_W - 1) // 2
    pad_c = (SC_W - 1) // 2
    for r0 in range(0, tc, sub):
        acc = jnp.broadcast_to(ab_ref[...], (sub, GROUP))
        for j in range(CONF_W):
            acc = acc + aw_ref[j:j + 1, :] * ha_scr[r0 + HALO - pad_a + j:r0 + HALO - pad_a + j + sub, :]
        mu = jnp.mean(acc, axis=-1, keepdims=True)
        xc = acc - mu
        y = xc * lax.rsqrt(jnp.mean(xc * xc, axis=-1, keepdims=True) + EPS) * lng_ref[...] + lnb_ref[...]
        y = y * _sigmoid(y)
        ya_ref[0, r0:r0 + sub, :] = _rms(y, gba_ref[...]).astype(BF16)

        acc = jnp.broadcast_to(cb_ref[...], (sub, GROUP))
        for j in range(SC_W):
            acc = acc + cw_ref[j:j + 1, :] * hc_scr[r0 + HALO - pad_c + j:r0 + HALO - pad_c + j + sub, :]
        y = uc_ref[0, r0:r0 + sub, 0:GROUP] * acc
        yc_ref[0, r0:r0 + sub, :] = _rms(y, gbc_ref[...]).astype(BF16)


def _conv_mixers(ua, uc, aw, ab, lng, lnb, cw, cb, gba, gbc, tc):
    b, s, _ = ua.shape
    nb = tc // HALO
    last = s // HALO - 1
    main = lambda w: pl.BlockSpec((1, tc, w), lambda bi, i: (bi, i, 0))
    prev = lambda w: pl.BlockSpec((1, HALO, w), lambda bi, i: (bi, jnp.maximum(i * nb - 1, 0), 0))
    nxt = lambda w: pl.BlockSpec((1, HALO, w), lambda bi, i: (bi, jnp.minimum((i + 1) * nb, last), 0))
    wa, wc = ua.shape[2], uc.shape[2]
    return pl.pallas_call(
        functools.partial(_conv_kernel, sub=64),
        grid=(b, s // tc),
        in_specs=[main(wa), prev(wa), nxt(wa), main(wc), prev(wc), nxt(wc),
                  _full(aw.shape), _full(ab.shape), _full(lng.shape), _full(lnb.shape),
                  _full(cw.shape), _full(cb.shape), _full(gba.shape), _full(gbc.shape)],
        out_specs=[pl.BlockSpec((1, tc, GROUP), lambda bi, i: (bi, i, 0))] * 2,
        out_shape=[jax.ShapeDtypeStruct((b, s, GROUP), BF16)] * 2,
        scratch_shapes=[pltpu.VMEM((tc + 2 * HALO, GROUP), F32)] * 2,
        compiler_params=_params("parallel", "parallel"),
        name="conv_mixers",
    )(ua, ua, ua, uc, uc, uc, aw, ab, lng, lnb, cw, cb, gba, gbc)


def _hgrn_kernel(gam_ref, qf_ref, zf_ref, vf_ref, qb_ref, zb_ref, vb_ref, of_ref, ob_ref,
                 st_f, st_b, b_f, b_b, c_f, c_b, lam_f, lam_b, qe_f, qe_b, kt_f, kt_b,
                 upd_f, upd_b, sts_f, sts_b, ones_scr, *, layer):
    i = pl.program_id(1)
    tb = qf_ref.shape[1]
    nst = tb // STEP
    half = STEP // 2
    cs = 256

    @pl.when(i == 0)
    def _():
        st_f[...] = jnp.zeros_like(st_f)
        st_b[...] = jnp.zeros_like(st_b)

    ones_scr[...] = jnp.where(_head_ones(GROUP, HDIM), 1.0, 0.0).astype(BF16)

    ti = lax.broadcasted_iota(jnp.int32, (cs, cs), 0)
    si = lax.broadcasted_iota(jnp.int32, (cs, cs), 1)
    same = (ti // STEP) == (si // STEP)
    m_all = jnp.where(same, 1.0, 0.0).astype(BF16)

    def prep(d, q_ref, z_ref, b_scr, c_scr, lam_scr, qe_scr, kt_scr):
        g = gam_ref[d]
        e = jnp.exp(g - jnp.max(g, axis=0, keepdims=True))
        tot = jnp.sum(e, axis=0, keepdims=True)
        if layer == 0:
            lb = jnp.zeros_like(tot)
        else:
            lb = jnp.sum(e[1:layer + 1], axis=0, keepdims=True) / tot
        tri = (si <= ti) if d == 0 else (si >= ti)
        m_tri = jnp.where(same & tri, 1.0, 0.0).astype(BF16)
        for r0 in range(0, tb, cs):
            z = z_ref[0, r0:r0 + cs, :]
            a = jnp.exp(-jnp.abs(z))
            big = 1.0 / (1.0 + a)
            pos = z >= 0.0
            f = lb + (1.0 - lb) * jnp.where(pos, big, a * big)
            kk = (1.0 - lb) * jnp.where(pos, a * big, big)
            lf = jnp.log2(f)
            bloc = _split_dot(m_tri, lf)
            btot = _split_dot(m_all, lf)
            b_scr[r0:r0 + cs, :] = bloc
            c_scr[r0:r0 + cs, :] = bloc - jnp.log2(kk)
            lam_scr[r0:r0 + cs, :] = jnp.exp2(btot)
            qe_scr[r0:r0 + cs, :] = (q_ref[0, r0:r0 + cs, :] * jnp.exp2(bloc)).astype(BF16)
            kt_scr[r0:r0 + cs, :] = (kk * jnp.exp2(btot - bloc)).astype(BF16)

    prep(0, qf_ref, zf_ref, b_f, c_f, lam_f, qe_f, kt_f)
    prep(1, qb_ref, zb_ref, b_b, c_b, lam_b, qe_b, kt_b)

    trow = lax.broadcasted_iota(jnp.int32, (half, GROUP), 0)
    head_of_lane = lax.broadcasted_iota(jnp.int32, (STEP, GROUP), 1) // HDIM

    def liveness(s, rev):
        if rev:
            return (1, 0) if s < half else (2, 1)
        return (1, 2) if s < half else (0, 1)

    def per_head_rows(x):
        zero = jnp.zeros_like(x)
        return jnp.concatenate([jnp.where(head_of_lane == h, x, zero) for h in range(HEADS)], axis=0)

    def increment(k, r, v_ref, kt_scr, upd_scr):
        vb = v_ref[0, pl.ds(r, STEP), :].astype(BF16)
        vstack = jnp.concatenate([vb[:, h * HDIM:(h + 1) * HDIM] for h in range(HEADS)], axis=0)
        upd_scr[pl.ds(k * HDIM, HDIM), :] = _dot_tn(vstack, per_head_rows(kt_scr[pl.ds(r, STEP), :]))

    def scan(k, r, st, lam_scr, upd_scr, sts_scr):
        blk = pl.ds(k * HDIM, HDIM)
        state = st[...]
        sts_scr[blk, :] = state.astype(BF16)
        st[...] = lam_scr[pl.ds(r, 1), :] * state + upd_scr[blk, :]

    def output(k, r, rev, q_ref, v_ref, o_ref, b_scr, c_scr, qe_scr, sts_scr):
        res = _dot_nt(per_head_rows(qe_scr[pl.ds(r, STEP), :]), sts_scr[pl.ds(k * HDIM, HDIM), :])
        o = jnp.concatenate([res[h * STEP:(h + 1) * STEP, :] for h in range(HEADS)], axis=1)
        o = [o[:half], o[half:]]
        q = (q_ref[0, pl.ds(r, half), :], q_ref[0, pl.ds(r + half, half), :])
        b = (b_scr[pl.ds(r, half), :], b_scr[pl.ds(r + half, half), :])
        pieces, where_to = [], []
        for s in range(STEP):
            crow = jnp.broadcast_to(c_scr[pl.ds(r + s, 1), :], (half, GROUP))
            for hx, kind in enumerate(liveness(s, rev)):
                if kind == 0:
                    continue
                dec = q[hx] * jnp.exp2(b[hx] - crow)
                if kind == 1:
                    sl = s - hx * half
                    dec = jnp.where((trow <= sl) if rev else (trow >= sl), dec, 0.0)
                pieces.append(dec)
                where_to.append((s, hx))
        att = _dot(jnp.concatenate(pieces, axis=0).astype(BF16), ones_scr[...])
        vrows = [jnp.broadcast_to(v_ref[0, pl.ds(r + s, 1), :], (half, GROUP)) for s in range(STEP)]
        for n, (s, hx) in enumerate(where_to):
            o[hx] = o[hx] + att[n * half:(n + 1) * half, :] * vrows[s]
        o_ref[0, pl.ds(r, half), :] = o[0]
        o_ref[0, pl.ds(r + half, half), :] = o[1]

    fwd_row = lambda k: pl.multiple_of(k * STEP, STEP)
    bwd_row = lambda k: pl.multiple_of((nst - 1 - k) * STEP, STEP)

    def increment_body(k, carry):
        increment(k, fwd_row(k), vf_ref, kt_f, upd_f)
        increment(k, bwd_row(k), vb_ref, kt_b, upd_b)
        return carry

    def scan_body(k, carry):
        scan(k, fwd_row(k), st_f, lam_f, upd_f, sts_f)
        scan(k, bwd_row(k), st_b, lam_b, upd_b, sts_b)
        return carry

    def output_body(k, carry):
        output(k, fwd_row(k), False, qf_ref, vf_ref, of_ref, b_f, c_f, qe_f, sts_f)
        output(k, bwd_row(k), True, qb_ref, vb_ref, ob_ref, b_b, c_b, qe_b, sts_b)
        return carry

    lax.fori_loop(0, nst, increment_body, 0, unroll=8)
    lax.fori_loop(0, nst, scan_body, 0, unroll=2)
    lax.fori_loop(0, nst, output_body, 0, unroll=4)


def _hgrn(ub, gamma, layer, tb):
    b, s, _ = ub.shape
    nb = s // tb
    col = lambda j, rev: pl.BlockSpec(
        (1, tb, GROUP), (lambda bi, i: (bi, nb - 1 - i, j)) if rev else (lambda bi, i: (bi, i, j)))
    vm = lambda dt: pltpu.VMEM((tb, GROUP), dt)
    return pl.pallas_call(
        functools.partial(_hgrn_kernel, layer=layer),
        grid=(b, nb),
        in_specs=[_full(gamma.shape),
                  col(0, False), col(1, False), col(3, False),
                  col(0, True), col(2, True), col(3, True)],
        out_specs=[pl.BlockSpec((1, tb, GROUP), lambda bi, i: (bi, i, 0)),
                   pl.BlockSpec((1, tb, GROUP), lambda bi, i: (bi, nb - 1 - i, 0))],
        out_shape=[jax.ShapeDtypeStruct((b, s, GROUP), F32)] * 2,
        scratch_shapes=[pltpu.VMEM((HDIM, GROUP), F32)] * 2
        + [vm(F32)] * 6 + [vm(BF16)] * 4
        + [pltpu.VMEM((tb // STEP * HDIM, GROUP), F32)] * 2
        + [pltpu.VMEM((tb // STEP * HDIM, GROUP), BF16)] * 2
        + [pltpu.VMEM((GROUP, GROUP), BF16)],
        compiler_params=_params("parallel", "arbitrary"),
        name="hgrn2",
    )(gamma, ub, ub, ub, ub, ub, ub)


def _mla_prep_kernel(ud_ref, ct_ref, sg_ref, qag_ref, kvag_ref, wq_ref, wk_ref,
                     gq_ref, gqs_ref, gk_ref, gks_ref, q_ref, k_ref, v_ref):
    ud = ud_ref[...]
    nw = HEADS * HPAD
    nq = _rms(ud[:, :GROUP], qag_ref[...]).astype(BF16)
    qq = _dot(nq, wq_ref[...])
    nkv = _rms(ud[:, GROUP:GROUP + HPAD], kvag_ref[...]).astype(BF16)
    kr = ud[:, GROUP + HPAD:]
    kr_hi = kr.astype(BF16)
    kr_lo = (kr - kr_hi.astype(F32)).astype(BF16)
    kk = _dot(jnp.concatenate([nkv, kr_hi, kr_lo], axis=-1), wk_ref[...])
    ct = ct_ref[...]
    sg = sg_ref[...]
    scale = QK ** -0.5 * LOG2E
    for h in range(HEADS):
        sl = slice(h * HPAD, (h + 1) * HPAD)
        sw = slice(nw + h * HPAD, nw + (h + 1) * HPAD)
        x = qq[:, sl]
        r = lax.rsqrt(jnp.sum(x * x, axis=-1, keepdims=True) * (1.0 / QK) + EPS) * scale
        q_ref[:, sl] = (r * (x * (gq_ref[...] * ct) + qq[:, sw] * (gqs_ref[...] * sg))).astype(BF16)
        x = kk[:, sl]
        r = lax.rsqrt(jnp.sum(x * x, axis=-1, keepdims=True) * (1.0 / QK) + EPS)
        k_ref[:, sl] = (r * (x * (gk_ref[...] * ct) + kk[:, sw] * (gks_ref[...] * sg))).astype(BF16)
    v_ref[...] = kk[:, 2 * nw:].astype(BF16)


def _mla_prep(ud, ct, sg, qag, kvag, wq, wk, gq, gqs, gk, gks, tm):
    t, w = ud.shape
    nw = HEADS * HPAD
    row = lambda n: pl.BlockSpec((tm, n), lambda i: (i, 0))
    return pl.pallas_call(
        _mla_prep_kernel,
        grid=(t // tm,),
        in_specs=[row(w), row(HPAD), row(HPAD)] + [_full(a.shape) for a in (qag, kvag, wq, wk, gq, gqs, gk, gks)],
        out_specs=[row(nw), row(nw), row(GROUP)],
        out_shape=[jax.ShapeDtypeStruct((t, nw), BF16), jax.ShapeDtypeStruct((t, nw), BF16),
                   jax.ShapeDtypeStruct((t, GROUP), BF16)],
        compiler_params=_params("parallel"),
        name="mla_prep",
    )(ud, ct, sg, qag, kvag, wq, wk, gq, gqs, gk, gks)


def _mla_attn_kernel(q_ref, k_ref, v_ref, gb_ref, o_ref):
    tq = q_ref.shape[1]
    lane = lax.broadcasted_iota(jnp.int32, (tq, GROUP), 1) // HDIM

    def scores(h):
        sl = slice(h * HPAD, (h + 1) * HPAD)
        return _dot_nt(q_ref[0, :, sl], k_ref[0, :, sl])

    def weighted(s):
        p = jnp.exp2(s - jnp.max(s, axis=-1, keepdims=True))
        return _dot(p.astype(BF16), v_ref[0]) / jnp.sum(p, axis=-1, keepdims=True)

    out = jnp.zeros((tq, GROUP), F32)
    s_next = scores(0)
    for h in range(HEADS):
        s = s_next
        if h + 1 < HEADS:
            s_next = scores(h + 1)
        out = jnp.where(lane == h, weighted(s), out)
    o_ref[0] = _rms(out, gb_ref[...]).astype(BF16)


def _mla_attn(q, k, v, gb, tq):
    b, s, nw = q.shape
    return pl.pallas_call(
        _mla_attn_kernel,
        grid=(b, s // tq),
        in_specs=[pl.BlockSpec((1, tq, nw), lambda bi, i: (bi, i, 0)),
                  pl.BlockSpec((1, s, nw), lambda bi, i: (bi, 0, 0)),
                  pl.BlockSpec((1, s, GROUP), lambda bi, i: (bi, 0, 0)),
                  _full(gb.shape)],
        out_specs=pl.BlockSpec((1, tq, GROUP), lambda bi, i: (bi, i, 0)),
        out_shape=jax.ShapeDtypeStruct((b, s, GROUP), BF16),
        compiler_params=_params("parallel", "arbitrary"),
        name="mla_attn",
    )(q, k, v, gb)


def _mix_out_kernel(x_ref, ya_ref, of_ref, ob_ref, g_ref, yc_ref, yd_ref, on_ref, gbb_ref, wo_ref,
                    gxq_ref, wq_ref, qn_ref, kx_ref, vx_ref, xwo_ref, o_ref):
    tm = x_ref.shape[1]
    ones_bd = jnp.where(_head_ones(GROUP, HDIM), 1.0, 0.0).astype(BF16)
    o = of_ref[0] + ob_ref[0]
    g = g_ref[0]
    yb = o * _head_rsqrt(o, ones_bd, HDIM) * on_ref[...] * (g * _sigmoid(g))
    yb = _rms(yb, gbb_ref[...]).astype(BF16)
    x1 = x_ref[0]
    for j, y in enumerate((ya_ref[0], yb, yc_ref[0], yd_ref[0])):
        x1 = x1 + _dot(y, wo_ref[j * GROUP:(j + 1) * GROUP, :])

    q = _dot(_rms(x1, gxq_ref[...]).astype(BF16), wq_ref[...])
    qn = q * _head_rsqrt(q, ones_bd, HDIM) * (qn_ref[...] * HDIM ** -0.5)
    lane = lax.broadcasted_iota(jnp.int32, (tm, GROUP), 1) // HDIM
    acc = jnp.zeros((tm, GROUP), F32)
    for h in range(HEADS):
        s = _dot_nt(jnp.where(lane == h, qn, 0.0).astype(BF16), kx_ref[0, 0])
        p = jnp.exp(s - jnp.max(s, axis=-1, keepdims=True))
        l = jnp.sum(p, axis=-1, keepdims=True)
        pv = _dot(p.astype(BF16), vx_ref[0, 0])
        acc = jnp.where(lane == h, pv / l, acc)
    o_ref[0] = x1 + _dot(acc.astype(BF16), xwo_ref[...])


def _mix_out(x, ya, of, ob, ub, yc, yd, on, gbb, wo, gxq, wq, qn, kx, vx, xwo, layer, tm):
    b, s, d = x.shape
    n = kx.shape[2]
    row = lambda w, j=0: pl.BlockSpec((1, tm, w), lambda bi, i: (bi, i, j))
    return pl.pallas_call(
        _mix_out_kernel,
        grid=(b, s // tm),
        in_specs=[row(d), row(GROUP), row(GROUP), row(GROUP), row(GROUP, 4), row(GROUP), row(GROUP),
                  _full(on.shape), _full(gbb.shape), _full(wo.shape), _full(gxq.shape), _full(wq.shape),
                  _full(qn.shape),
                  pl.BlockSpec((1, 1, n, GROUP), lambda bi, i: (layer, bi, 0, 0)),
                  pl.BlockSpec((1, 1, n, GROUP), lambda bi, i: (layer, bi, 0, 0)),
                  _full(xwo.shape)],
        out_specs=row(d),
        out_shape=jax.ShapeDtypeStruct((b, s, d), F32),
        compiler_params=_params("parallel", "parallel"),
        name="mix_out",
    )(x, ya, of, ob, ub, yc, yd, on, gbb, wo, gxq, wq, qn, kx, vx, xwo)


def _ffn_kernel(x_ref, g_ref, w13_ref, w2_ref, o_ref):
    x = x_ref[...]
    dff = w2_ref.shape[0]
    a = _dot(_rms(x, g_ref[...]).astype(BF16), w13_ref[...])
    a1 = a[:, :dff]
    o_ref[...] = x + _dot((a1 * _sigmoid(a1) * a[:, dff:]).astype(BF16), w2_ref[...])


def _ffn(x2, g, w13, w2, tm):
    t, d = x2.shape
    single = dict(pipeline_mode=pl.Buffered(1))
    return pl.pallas_call(
        _ffn_kernel,
        grid=(t // tm,),
        in_specs=[pl.BlockSpec((tm, d), lambda i: (i, 0)), _full((1, d)),
                  pl.BlockSpec(w13.shape, lambda i: (0, 0), **single),
                  pl.BlockSpec(w2.shape, lambda i: (0, 0), **single)],
        out_specs=pl.BlockSpec((tm, d), lambda i: (i, 0)),
        out_shape=jax.ShapeDtypeStruct((t, d), F32),
        compiler_params=_params("parallel"),
        name="ffn",
    )(x2, g, w13, w2)


def _pad_heads(w, width):
    lead = w.shape[:-1]
    w = w.reshape(lead + (HEADS, width))
    w = jnp.pad(w, [(0, 0)] * len(lead) + [(0, 0), (0, HPAD - width)])
    return w.reshape(lead + (HEADS * HPAD,))


def _swap_rope(w):
    half = ROPE // 2
    z = jnp.zeros_like(w[..., :HDIM])
    return jnp.concatenate([z, w[..., HDIM + half:], w[..., HDIM:HDIM + half]], axis=-1)


def _mla_weights(wuq, wukv, qn_g, kn_g):
    dq = wuq.shape[0]
    dkv = wukv.shape[0]
    wq3 = wuq.reshape(dq, HEADS, QK)
    wq = jnp.concatenate([_pad_heads(wuq, QK), _pad_heads(_swap_rope(wq3).reshape(dq, -1), QK)], axis=1)
    kv3 = wukv.reshape(dkv, HEADS, 2 * HDIM)
    k_nope = jnp.pad(kv3[..., :HDIM], [(0, 0), (0, 0), (0, HPAD - HDIM)]).reshape(dkv, -1)
    v_cols = kv3[..., HDIM:].reshape(dkv, -1)
    top = jnp.concatenate([k_nope, jnp.zeros_like(k_nope), v_cols], axis=1)
    eye = jnp.eye(ROPE, dtype=F32)
    place = jnp.pad(eye, [(0, 0), (HDIM, HPAD - QK)])
    place_sw = jnp.pad(jnp.roll(eye, ROPE // 2, axis=1), [(0, 0), (HDIM, HPAD - QK)])
    rope_rows = jnp.concatenate([jnp.tile(place, (1, HEADS)), jnp.tile(place_sw, (1, HEADS)),
                                 jnp.zeros((ROPE, v_cols.shape[1]), F32)], axis=1)
    rope_rows = jnp.pad(rope_rows, [(0, HPAD - ROPE), (0, 0)])
    wk = jnp.concatenate([top, rope_rows, rope_rows], axis=0)
    pad1 = lambda g: jnp.pad(g, (0, HPAD - QK))[None, :]
    return (wq.astype(BF16), wk.astype(BF16),
            pad1(qn_g), pad1(_swap_rope(qn_g)), pad1(kn_g), pad1(_swap_rope(kn_g)))


def kernel(x, mem, positions, g_mix, w_in, a_dw_w, a_dw_b, a_ln_g, a_ln_b, h_gamma, h_onorm_g, c_dw_w, c_dw_b, m_qa_g, m_wuq, m_kva_g, m_wukv, m_qn_g, m_kn_g, g_branch, w_out, g_xq, g_mem, x_wq, x_wkv, x_qn_g, x_kn_g, x_wo, g_ffn, f_w13, f_w2):
    b, s, d = x.shape
    t = b * s
    depth = w_in.shape[0]
    row = lambda v: v[None, :]
    tile4 = lambda v: jnp.tile(v, HEADS)[None, :]

    half = ROPE // 2
    inv = ROPE_BASE ** (-jnp.arange(0, ROPE, 2, dtype=F32) / ROPE)
    inv_row = jnp.concatenate([jnp.zeros((HDIM,), F32), inv, inv, jnp.zeros((HPAD - QK,), F32)])[None, :]
    ct, sg = _rope_tables(positions.astype(F32).reshape(t, 1), inv_row, 512)

    n_mem = mem.shape[1]
    kv4 = x_wkv.reshape(depth, d, HEADS, 2, HDIM).transpose(0, 1, 3, 2, 4).reshape(depth, d, 2 * GROUP)
    kx, vx = _mem_kv(mem, g_mem[:, None, :], kv4.astype(BF16), jnp.tile(x_kn_g, (1, HEADS))[:, None, :])

    widths = (2 * GROUP, 5 * GROUP, 3 * GROUP, 2 * GROUP)
    n_in = w_in.shape[2]
    for l in range(depth):
        w_in_l = jnp.pad(w_in[l], [(0, 0), (0, sum(widths) - n_in)]).astype(BF16)
        ua, ub, uc, ud = _in_proj(x.reshape(t, d), row(g_mix[l]), w_in_l, widths, 512)
        ua = ua.reshape(b, s, -1)
        ub = ub.reshape(b, s, -1)
        uc = uc.reshape(b, s, -1)
        gb = g_branch[l]
        ya, yc = _conv_mixers(ua, uc, a_dw_w[l], row(a_dw_b[l]), row(a_ln_g[l]), row(a_ln_b[l]),
                              c_dw_w[l], row(c_dw_b[l]), row(gb[:GROUP]), row(gb[2 * GROUP:3 * GROUP]), 512)
        of, ob = _hgrn(ub, h_gamma, l, 512)
        wq, wk, gq, gqs, gk, gks = _mla_weights(m_wuq[l], m_wukv[l], m_qn_g[l], m_kn_g[l])
        qr, kr, vr = _mla_prep(ud, ct, sg, row(m_qa_g[l]), row(m_kva_g[l]), wq, wk, gq, gqs, gk, gks, 512)
        yd = _mla_attn(qr.reshape(b, s, -1), kr.reshape(b, s, -1), vr.reshape(b, s, -1),
                       row(gb[3 * GROUP:]), 256)
        x = _mix_out(x, ya, of, ob, ub, yc, yd, row(h_onorm_g[l]), row(gb[GROUP:2 * GROUP]),
                     w_out[l].astype(BF16), row(g_xq[l]), x_wq[l].astype(BF16), tile4(x_qn_g[l]),
                     kx, vx, x_wo[l].astype(BF16), l, 256)
        x = _ffn(x.reshape(t, d), row(g_ffn[l]), f_w13[l].astype(BF16), f_w2[l].astype(BF16), 256).reshape(b, s, d)
    return x
```

```python
import functools

import jax
import jax.numpy as jnp
from jax import lax
from jax.experimental import pallas as pl
from jax.experimental.pallas import tpu as pltpu

F32 = jnp.float32
BF16 = jnp.bfloat16
EPS = 1e-6

GROUP = 256
HEADS = 4
HDIM = 64
ROPE = 32
QK = HDIM + ROPE
HPAD = 128
CONF_W = 31
SC_W = 3
HALO = 16
STEP = 16
SUBLANES = 8
ROPE_BASE = 10000.0
LOG2E = 1.4426950408889634
VMEM_LIMIT = 60 * 1024 * 1024


def _rms(x, g):
    return x * lax.rsqrt(jnp.mean(x * x, axis=-1, keepdims=True) + EPS) * g


def _sigmoid(x):
    return 1.0 / (1.0 + jnp.exp(-x))


def _dot(a, b):
    return jnp.dot(a, b, preferred_element_type=F32)


def _dot_nt(a, b):
    return lax.dot_general(a, b, (((1,), (1,)), ((), ())), preferred_element_type=F32)


def _dot_tn(a, b):
    return lax.dot_general(a, b, (((0,), (0,)), ((), ())), preferred_element_type=F32)


def _split_dot(m, x):
    hi = x.astype(BF16)
    lo = (x - hi.astype(F32)).astype(BF16)
    return _dot(m, hi) + _dot(m, lo)


def _head_ones(n, width):
    r = lax.broadcasted_iota(jnp.int32, (n, n), 0) // width
    c = lax.broadcasted_iota(jnp.int32, (n, n), 1) // width
    return r == c


def _head_rsqrt(x, ones_bd, width):
    return lax.rsqrt(_split_dot_rhs(x * x, ones_bd) * (1.0 / width) + EPS)


def _split_dot_rhs(x, m):
    hi = x.astype(BF16)
    lo = (x - hi.astype(F32)).astype(BF16)
    return _dot(hi, m) + _dot(lo, m)


def _params(*sem):
    return pltpu.CompilerParams(dimension_semantics=sem, vmem_limit_bytes=VMEM_LIMIT)


def _full(shape):
    nd = len(shape)
    return pl.BlockSpec(shape, lambda *_: (0,) * nd)


def _rope_kernel(pos_ref, inv_ref, ct_ref, sg_ref):
    ang = pos_ref[...] * inv_ref[...]
    lane = lax.broadcasted_iota(jnp.int32, ang.shape, 1)
    c = jnp.cos(ang)
    s = jnp.sin(ang)
    half = ROPE // 2
    ct_ref[...] = jnp.where(lane < HDIM, 1.0, jnp.where(lane < QK, c, 0.0))
    sg_ref[...] = jnp.where(lane < HDIM, 0.0,
                            jnp.where(lane < HDIM + half, -s, jnp.where(lane < QK, s, 0.0)))


def _rope_tables(pos_f, inv_row, tm):
    t = pos_f.shape[0]
    return pl.pallas_call(
        _rope_kernel,
        grid=(t // tm,),
        in_specs=[pl.BlockSpec((tm, 1), lambda i: (i, 0)), _full((1, HPAD))],
        out_specs=[pl.BlockSpec((tm, HPAD), lambda i: (i, 0))] * 2,
        out_shape=[jax.ShapeDtypeStruct((t, HPAD), F32)] * 2,
        compiler_params=_params("parallel"),
        name="rope_tables",
    )(pos_f, inv_row)


def _memkv_kernel(mem_ref, g_ref, w_ref, kn_ref, k_ref, v_ref):
    mn = _rms(mem_ref[0], g_ref[0]).astype(BF16)
    kv = _dot(mn, w_ref[0])
    k = kv[:, :GROUP]
    ones_bd = jnp.where(_head_ones(GROUP, HDIM), 1.0, 0.0).astype(BF16)
    k_ref[0, 0] = (k * _head_rsqrt(k, ones_bd, HDIM) * kn_ref[0]).astype(BF16)
    v_ref[0, 0] = kv[:, GROUP:].astype(BF16)


def _mem_kv(mem, g_mem, wkv, kn_g):
    b, n, d = mem.shape
    nl = wkv.shape[0]
    return pl.pallas_call(
        _memkv_kernel,
        grid=(nl, b),
        in_specs=[pl.BlockSpec((1, n, d), lambda l, i: (i, 0, 0)),
                  pl.BlockSpec((1, 1, d), lambda l, i: (l, 0, 0)),
                  pl.BlockSpec((1, d, 2 * GROUP), lambda l, i: (l, 0, 0)),
                  pl.BlockSpec((1, 1, GROUP), lambda l, i: (l, 0, 0))],
        out_specs=[pl.BlockSpec((1, 1, n, GROUP), lambda l, i: (l, i, 0, 0))] * 2,
        out_shape=[jax.ShapeDtypeStruct((nl, b, n, GROUP), BF16)] * 2,
        compiler_params=_params("parallel", "parallel"),
        name="mem_kv",
    )(mem, g_mem, wkv, kn_g)


def _inproj_kernel(x_ref, g_ref, w_ref, ua_ref, ub_ref, uc_ref, ud_ref):
    n = _rms(x_ref[...], g_ref[...]).astype(BF16)
    c = 0
    for ref in (ua_ref, ub_ref, uc_ref, ud_ref):
        w = ref.shape[1]
        ref[...] = _dot(n, w_ref[:, c:c + w])
        c += w


def _in_proj(x2, g, w, widths, tm):
    t, d = x2.shape
    return pl.pallas_call(
        _inproj_kernel,
        grid=(t // tm,),
        in_specs=[pl.BlockSpec((tm, d), lambda i: (i, 0)), _full((1, d)), _full(w.shape)],
        out_specs=[pl.BlockSpec((tm, n), lambda i: (i, 0)) for n in widths],
        out_shape=[jax.ShapeDtypeStruct((t, n), F32) for n in widths],
        compiler_params=_params("parallel"),
        name="in_proj",
    )(x2, g, w)


def _conv_parts(ua_ref, uap_ref, uan_ref, uc_ref, ucp_ref, ucn_ref,
                aw_ref, ab_ref, lng_ref, lnb_ref, cw_ref, cb_ref, gba_ref, gbc_ref,
                ya_ref, yc_ref, ha_scr, hc_scr, sub):
    tc = ua_ref.shape[1]
    pad_a = (CONF_W - 1) // 2
    pad_c = (SC_W - 1) // 2

    def glu(u):
        return u[:, :GROUP] * _sigmoid(u[:, GROUP:])

    def gated(u):
        return u[:, GROUP:2 * GROUP] * u[:, 2 * GROUP:]

    def fill():
        i = pl.program_id(1)
        has_prev = jnp.where(i > 0, 1.0, 0.0)
        has_next = jnp.where(i < pl.num_programs(1) - 1, 1.0, 0.0)
        ha_scr[0:HALO, :] = glu(uap_ref[0]) * has_prev
        ha_scr[HALO:HALO + tc, :] = glu(ua_ref[0])
        ha_scr[HALO + tc:, :] = glu(uan_ref[0]) * has_next
        hc_scr[0:HALO, :] = gated(ucp_ref[0]) * has_prev
        hc_scr[HALO:HALO + tc, :] = gated(uc_ref[0])
        hc_scr[HALO + tc:, :] = gated(ucn_ref[0]) * has_next

    def tile(r0):
        acc = jnp.broadcast_to(ab_ref[...], (sub, GROUP))
        for r in range(SUBLANES):
            part = None
            for j in range(CONF_W):
                start = HALO - pad_a + j
                if start % SUBLANES != r:
                    continue
                lo = r0 + start - r
                term = aw_ref[j:j + 1, :] * ha_scr[lo:lo + sub + SUBLANES, :]
                part = term if part is None else part + term
            if part is not None:
                acc = acc + part[r:r + sub, :]
        mu = jnp.mean(acc, axis=-1, keepdims=True)
        xc = acc - mu
        y = xc * lax.rsqrt(jnp.mean(xc * xc, axis=-1, keepdims=True) + EPS) * lng_ref[...] + lnb_ref[...]
        y = y * _sigmoid(y)
        ya_ref[0, r0:r0 + sub, :] = _rms(y, gba_ref[...]).astype(BF16)

        acc = jnp.broadcast_to(cb_ref[...], (sub, GROUP))
        for j in range(SC_W):
            acc = acc + cw_ref[j:j + 1, :] * hc_scr[r0 + HALO - pad_c + j:r0 + HALO - pad_c + j + sub, :]
        y = uc_ref[0, r0:r0 + sub, 0:GROUP] * acc
        yc_ref[0, r0:r0 + sub, :] = _rms(y, gbc_ref[...]).astype(BF16)

    return fill, tile


def _hgrn_parts(gam_ref, qf_ref, zf_ref, vf_ref, qb_ref, zb_ref, vb_ref, of_ref, ob_ref,
                st_f, st_b, b_f, b_b, c_f, c_b, lam_f, lam_b, qe_f, qe_b, kt_f, kt_b,
                upd_f, upd_b, sts_f, sts_b, ones_scr, layer):
    i = pl.program_id(1)
    tb = qf_ref.shape[1]
    nst = tb // STEP
    half = STEP // 2
    cs = 256

    @pl.when(i == 0)
    def _():
        st_f[...] = jnp.zeros_like(st_f)
        st_b[...] = jnp.zeros_like(st_b)

    ones_scr[...] = jnp.where(_head_ones(GROUP, HDIM), 1.0, 0.0).astype(BF16)

    ti = lax.broadcasted_iota(jnp.int32, (cs, cs), 0)
    si = lax.broadcasted_iota(jnp.int32, (cs, cs), 1)
    same = (ti // STEP) == (si // STEP)
    m_all = jnp.where(same, 1.0, 0.0).astype(BF16)

    def prep(d, q_ref, z_ref, b_scr, c_scr, lam_scr, qe_scr, kt_scr):
        g = gam_ref[d]
        e = jnp.exp(g - jnp.max(g, axis=0, keepdims=True))
        tot = jnp.sum(e, axis=0, keepdims=True)
        if layer == 0:
            lb = jnp.zeros_like(tot)
        else:
            lb = jnp.sum(e[1:layer + 1], axis=0, keepdims=True) / tot
        tri = (si <= ti) if d == 0 else (si >= ti)
        m_tri = jnp.where(same & tri, 1.0, 0.0).astype(BF16)
        for r0 in range(0, tb, cs):
            z = z_ref[0, r0:r0 + cs, :]
            a = jnp.exp(-jnp.abs(z))
            big = 1.0 / (1.0 + a)
            pos = z >= 0.0
            f = lb + (1.0 - lb) * jnp.where(pos, big, a * big)
            kk = (1.0 - lb) * jnp.where(pos, a * big, big)
            lf = jnp.log2(f)
            bloc = _split_dot(m_tri, lf)
            btot = _split_dot(m_all, lf)
            b_scr[r0:r0 + cs, :] = bloc
            c_scr[r0:r0 + cs, :] = bloc - jnp.log2(kk)
            lam_scr[r0:r0 + cs, :] = jnp.exp2(btot)
            qe_scr[r0:r0 + cs, :] = (q_ref[0, r0:r0 + cs, :] * jnp.exp2(bloc)).astype(BF16)
            kt_scr[r0:r0 + cs, :] = (kk * jnp.exp2(btot - bloc)).astype(BF16)

    prep(0, qf_ref, zf_ref, b_f, c_f, lam_f, qe_f, kt_f)
    prep(1, qb_ref, zb_ref, b_b, c_b, lam_b, qe_b, kt_b)

    trow = lax.broadcasted_iota(jnp.int32, (half, GROUP), 0)
    head_of_lane = lax.broadcasted_iota(jnp.int32, (STEP, GROUP), 1) // HDIM

    def liveness(s, rev):
        if rev:
            return (1, 0) if s < half else (2, 1)
        return (1, 2) if s < half else (0, 1)

    def per_head_rows(x):
        zero = jnp.zeros_like(x)
        return jnp.concatenate([jnp.where(head_of_lane == h, x, zero) for h in range(HEADS)], axis=0)

    def increment(k, r, v_ref, kt_scr, upd_scr):
        vb = v_ref[0, pl.ds(r, STEP), :].astype(BF16)
        vstack = jnp.concatenate([vb[:, h * HDIM:(h + 1) * HDIM] for h in range(HEADS)], axis=0)
        upd_scr[pl.ds(k * HDIM, HDIM), :] = _dot_tn(vstack, per_head_rows(kt_scr[pl.ds(r, STEP), :]))

    def scan(k, r, st, lam_scr, upd_scr, sts_scr):
        blk = pl.ds(k * HDIM, HDIM)
        state = st[...]
        sts_scr[blk, :] = state.astype(BF16)
        st[...] = lam_scr[pl.ds(r, 1), :] * state + upd_scr[blk, :]

    def output(k, r, rev, q_ref, v_ref, o_ref, b_scr, c_scr, qe_scr, sts_scr):
        res = _dot_nt(per_head_rows(qe_scr[pl.ds(r, STEP), :]), sts_scr[pl.ds(k * HDIM, HDIM), :])
        o = jnp.concatenate([res[h * STEP:(h + 1) * STEP, :] for h in range(HEADS)], axis=1)
        o = [o[:half], o[half:]]
        q = (q_ref[0, pl.ds(r, half), :], q_ref[0, pl.ds(r + half, half), :])
        b = (b_scr[pl.ds(r, half), :], b_scr[pl.ds(r + half, half), :])
        pieces, where_to = [], []
        for s in range(STEP):
            crow = jnp.broadcast_to(c_scr[pl.ds(r + s, 1), :], (half, GROUP))
            for hx, kind in enumerate(liveness(s, rev)):
                if kind == 0:
                    continue
                dec = q[hx] * jnp.exp2(b[hx] - crow)
                if kind == 1:
                    sl = s - hx * half
                    dec = jnp.where((trow <= sl) if rev else (trow >= sl), dec, 0.0)
                pieces.append(dec)
                where_to.append((s, hx))
        att = _dot(jnp.concatenate(pieces, axis=0).astype(BF16), ones_scr[...])
        vrows = [jnp.broadcast_to(v_ref[0, pl.ds(r + s, 1), :], (half, GROUP)) for s in range(STEP)]
        for n, (s, hx) in enumerate(where_to):
            o[hx] = o[hx] + att[n * half:(n + 1) * half, :] * vrows[s]
        o_ref[0, pl.ds(r, half), :] = o[0]
        o_ref[0, pl.ds(r + half, half), :] = o[1]

    fwd_row = lambda k: pl.multiple_of(k * STEP, STEP)
    bwd_row = lambda k: pl.multiple_of((nst - 1 - k) * STEP, STEP)

    def increment_body(k, carry):
        increment(k, fwd_row(k), vf_ref, kt_f, upd_f)
        increment(k, bwd_row(k), vb_ref, kt_b, upd_b)
        return carry

    def scan_body(k, carry):
        scan(k, fwd_row(k), st_f, lam_f, upd_f, sts_f)
        scan(k, bwd_row(k), st_b, lam_b, upd_b, sts_b)
        return carry

    def output_step(k):
        output(k, k * STEP, False, qf_ref, vf_ref, of_ref, b_f, c_f, qe_f, sts_f)
        output(k, (nst - 1 - k) * STEP, True, qb_ref, vb_ref, ob_ref, b_b, c_b, qe_b, sts_b)

    lax.fori_loop(0, nst, increment_body, 0, unroll=8)
    lax.fori_loop(0, nst, scan_body, 0, unroll=2)
    return nst, output_step


def _hgrn_scratch(tb):
    vm = lambda dt: pltpu.VMEM((tb, GROUP), dt)
    return ([pltpu.VMEM((HDIM, GROUP), F32)] * 2 + [vm(F32)] * 6 + [vm(BF16)] * 4
            + [pltpu.VMEM((tb // STEP * HDIM, GROUP), F32)] * 2
            + [pltpu.VMEM((tb // STEP * HDIM, GROUP), BF16)] * 2
            + [pltpu.VMEM((GROUP, GROUP), BF16)])


def _mla_prep_kernel(ud_ref, ct_ref, sg_ref, qag_ref, kvag_ref, wq_ref, wk_ref,
                     gq_ref, gqs_ref, gk_ref, gks_ref, q_ref, k_ref, v_ref):
    ud = ud_ref[...]
    nw = HEADS * HPAD
    nq = _rms(ud[:, :GROUP], qag_ref[...]).astype(BF16)
    qq = _dot(nq, wq_ref[...])
    nkv = _rms(ud[:, GROUP:GROUP + HPAD], kvag_ref[...]).astype(BF16)
    kr = ud[:, GROUP + HPAD:]
    kr_hi = kr.astype(BF16)
    kr_lo = (kr - kr_hi.astype(F32)).astype(BF16)
    kk = _dot(jnp.concatenate([nkv, kr_hi, kr_lo], axis=-1), wk_ref[...])
    ct = ct_ref[...]
    sg = sg_ref[...]
    scale = QK ** -0.5 * LOG2E
    for h in range(HEADS):
        sl = slice(h * HPAD, (h + 1) * HPAD)
        sw = slice(nw + h * HPAD, nw + (h + 1) * HPAD)
        x = qq[:, sl]
        r = lax.rsqrt(jnp.sum(x * x, axis=-1, keepdims=True) * (1.0 / QK) + EPS) * scale
        q_ref[:, sl] = (r * (x * (gq_ref[...] * ct) + qq[:, sw] * (gqs_ref[...] * sg))).astype(BF16)
        x = kk[:, sl]
        r = lax.rsqrt(jnp.sum(x * x, axis=-1, keepdims=True) * (1.0 / QK) + EPS)
        k_ref[:, sl] = (r * (x * (gk_ref[...] * ct) + kk[:, sw] * (gks_ref[...] * sg))).astype(BF16)
    first = lax.broadcasted_iota(jnp.int32, (ud.shape[0], HPAD), 1) < HDIM
    for pair in range(HEADS // 2):
        vv = kk[:, 2 * nw + pair * HPAD:2 * nw + (pair + 1) * HPAD]
        v_ref[2 * pair] = jnp.where(first, vv, 1.0).astype(BF16)
        v_ref[2 * pair + 1] = jnp.where(first, 1.0, vv).astype(BF16)


def _mla_prep(ud, ct, sg, qag, kvag, wq, wk, gq, gqs, gk, gks, tm):
    t, w = ud.shape
    nw = HEADS * HPAD
    row = lambda n: pl.BlockSpec((tm, n), lambda i: (i, 0))
    return pl.pallas_call(
        _mla_prep_kernel,
        grid=(t // tm,),
        in_specs=[row(w), row(HPAD), row(HPAD)] + [_full(a.shape) for a in (qag, kvag, wq, wk, gq, gqs, gk, gks)],
        out_specs=[row(nw), row(nw), pl.BlockSpec((HEADS, tm, HPAD), lambda i: (0, i, 0))],
        out_shape=[jax.ShapeDtypeStruct((t, nw), BF16), jax.ShapeDtypeStruct((t, nw), BF16),
                   jax.ShapeDtypeStruct((HEADS, t, HPAD), BF16)],
        compiler_params=_params("parallel"),
        name="mla_prep",
    )(ud, ct, sg, qag, kvag, wq, wk, gq, gqs, gk, gks)


def _attn_parts(q_ref, k_ref, v_ref, gb_ref, o_ref, rows):
    tq = q_ref.shape[1]
    first = lax.broadcasted_iota(jnp.int32, (rows, HPAD), 1) < HDIM
    items = [(r0, h) for r0 in range(0, tq, rows) for h in range(HEADS)]
    done = []

    def scores(item):
        r0, h = item
        sl = slice(h * HPAD, (h + 1) * HPAD)
        return _dot_nt(q_ref[0, r0:r0 + rows, sl], k_ref[0, :, sl])

    def finish(n, s):
        r0, h = items[n]
        sb = s.astype(BF16)
        p = jnp.exp2(sb - jnp.max(sb, axis=-1, keepdims=True))
        w = _dot(p, v_ref[h, 0])
        w = w / pltpu.roll(w, HDIM, axis=1)
        done.append(w)
        if h == HEADS - 1:
            pairs = [jnp.where(first, done[2 * j], done[2 * j + 1]) for j in range(HEADS // 2)]
            del done[:]
            o_ref[0, r0:r0 + rows, :] = _rms(jnp.concatenate(pairs, axis=1), gb_ref[...]).astype(BF16)

    return items, scores, finish


N_HGRN_SCRATCH = 17


def _mixers_kernel(*refs, layer, rows, conv_rows):
    (gam_ref, qf_ref, zf_ref, vf_ref, qb_ref, zb_ref, vb_ref,
     ua_ref, uap_ref, uan_ref, uc_ref, ucp_ref, ucn_ref,
     aw_ref, ab_ref, lng_ref, lnb_ref, cw_ref, cb_ref, gba_ref, gbc_ref,
     q_ref, k_ref, v_ref, gbd_ref,
     of_ref, ob_ref, ya_ref, yc_ref, yd_ref) = refs[:30]
    hgrn_scr = refs[30:30 + N_HGRN_SCRATCH]
    ha_scr, hc_scr = refs[30 + N_HGRN_SCRATCH:]

    conv_fill, conv_tile = _conv_parts(ua_ref, uap_ref, uan_ref, uc_ref, ucp_ref, ucn_ref,
                                       aw_ref, ab_ref, lng_ref, lnb_ref, cw_ref, cb_ref, gba_ref, gbc_ref,
                                       ya_ref, yc_ref, ha_scr, hc_scr, conv_rows)
    conv_fill()
    nst, hgrn_step = _hgrn_parts(gam_ref, qf_ref, zf_ref, vf_ref, qb_ref, zb_ref, vb_ref, of_ref, ob_ref,
                                 *hgrn_scr, layer)
    items, scores, finish = _attn_parts(q_ref, k_ref, v_ref, gbd_ref, yd_ref, rows)
    conv_tiles = list(range(0, ua_ref.shape[1], conv_rows))
    share = lambda seq, n: seq[n * len(seq) // len(items):(n + 1) * len(seq) // len(items)]

    s_next = scores(items[0])
    for n in range(len(items)):
        s = s_next
        if n + 1 < len(items):
            s_next = scores(items[n + 1])
        finish(n, s)
        for k in share(range(nst), n):
            hgrn_step(k)
        for r0 in share(conv_tiles, n):
            conv_tile(r0)


def _mixers(ua, ub, uc, q, k, v, gamma, aw, ab, lng, lnb, cw, cb, gba, gbc, gbd, layer, tile):
    b, s, _ = ub.shape
    nb = s // tile
    hb = tile // HALO
    last = s // HALO - 1
    col = lambda j, rev: pl.BlockSpec(
        (1, tile, GROUP), (lambda bi, i: (bi, nb - 1 - i, j)) if rev else (lambda bi, i: (bi, i, j)))
    main = lambda w: pl.BlockSpec((1, tile, w), lambda bi, i: (bi, i, 0))
    prev = lambda w: pl.BlockSpec((1, HALO, w), lambda bi, i: (bi, jnp.maximum(i * hb - 1, 0), 0))
    nxt = lambda w: pl.BlockSpec((1, HALO, w), lambda bi, i: (bi, jnp.minimum((i + 1) * hb, last), 0))
    whole = lambda w: pl.BlockSpec((1, s, w), lambda bi, i: (bi, 0, 0), pipeline_mode=pl.Buffered(1))
    wa, wc, nw = ua.shape[2], uc.shape[2], q.shape[2]
    small = [aw, ab, lng, lnb, cw, cb, gba, gbc]
    return pl.pallas_call(
        functools.partial(_mixers_kernel, layer=layer, rows=256, conv_rows=64),
        grid=(b, nb),
        in_specs=[_full(gamma.shape),
                  col(0, False), col(1, False), col(3, False), col(0, True), col(2, True), col(3, True),
                  main(wa), prev(wa), nxt(wa), main(wc), prev(wc), nxt(wc)]
        + [_full(a.shape) for a in small]
        + [main(nw), whole(nw),
           pl.BlockSpec((HEADS, 1, s, HPAD), lambda bi, i: (0, bi, 0, 0), pipeline_mode=pl.Buffered(1)),
           _full(gbd.shape)],
        out_specs=[main(GROUP), pl.BlockSpec((1, tile, GROUP), lambda bi, i: (bi, nb - 1 - i, 0)),
                   main(GROUP), main(GROUP), main(GROUP)],
        out_shape=[jax.ShapeDtypeStruct((b, s, GROUP), F32)] * 2 + [jax.ShapeDtypeStruct((b, s, GROUP), BF16)] * 3,
        scratch_shapes=_hgrn_scratch(tile) + [pltpu.VMEM((tile + 2 * HALO, GROUP), F32)] * 2,
        compiler_params=_params("parallel", "arbitrary"),
        name="mixers",
    )(gamma, ub, ub, ub, ub, ub, ub, ua, ua, ua, uc, uc, uc, *small, q, k, v, gbd)


def _mix_out_kernel(x_ref, ya_ref, of_ref, ob_ref, g_ref, yc_ref, yd_ref, on_ref, gbb_ref, wo_ref,
                    gxq_ref, wq_ref, qn_ref, kx_ref, vx_ref, xwo_ref, o_ref):
    tm = x_ref.shape[1]
    ones_bd = jnp.where(_head_ones(GROUP, HDIM), 1.0, 0.0).astype(BF16)
    o = of_ref[0] + ob_ref[0]
    g = g_ref[0]
    yb = o * _head_rsqrt(o, ones_bd, HDIM) * on_ref[...] * (g * _sigmoid(g))
    yb = _rms(yb, gbb_ref[...]).astype(BF16)
    x1 = x_ref[0]
    for j, y in enumerate((ya_ref[0], yb, yc_ref[0], yd_ref[0])):
        x1 = x1 + _dot(y, wo_ref[j * GROUP:(j + 1) * GROUP, :])

    q = _dot(_rms(x1, gxq_ref[...]).astype(BF16), wq_ref[...])
    qn = q * _head_rsqrt(q, ones_bd, HDIM) * (qn_ref[...] * HDIM ** -0.5)
    lane = lax.broadcasted_iota(jnp.int32, (tm, GROUP), 1) // HDIM
    acc = jnp.zeros((tm, GROUP), F32)
    for h in range(HEADS):
        s = _dot_nt(jnp.where(lane == h, qn, 0.0).astype(BF16), kx_ref[0, 0])
        p = jnp.exp(s - jnp.max(s, axis=-1, keepdims=True))
        l = jnp.sum(p, axis=-1, keepdims=True)
        pv = _dot(p.astype(BF16), vx_ref[0, 0])
        acc = jnp.where(lane == h, pv / l, acc)
    o_ref[0] = x1 + _dot(acc.astype(BF16), xwo_ref[...])


def _mix_out(x, ya, of, ob, ub, yc, yd, on, gbb, wo, gxq, wq, qn, kx, vx, xwo, layer, tm):
    b, s, d = x.shape
    n = kx.shape[2]
    row = lambda w, j=0: pl.BlockSpec((1, tm, w), lambda bi, i: (bi, i, j))
    return pl.pallas_call(
        _mix_out_kernel,
        grid=(b, s // tm),
        in_specs=[row(d), row(GROUP), row(GROUP), row(GROUP), row(GROUP, 4), row(GROUP), row(GROUP),
                  _full(on.shape), _full(gbb.shape), _full(wo.shape), _full(gxq.shape), _full(wq.shape),
                  _full(qn.shape),
                  pl.BlockSpec((1, 1, n, GROUP), lambda bi, i: (layer, bi, 0, 0)),
                  pl.BlockSpec((1, 1, n, GROUP), lambda bi, i: (layer, bi, 0, 0)),
                  _full(xwo.shape)],
        out_specs=row(d),
        out_shape=jax.ShapeDtypeStruct((b, s, d), F32),
        compiler_params=_params("parallel", "parallel"),
        name="mix_out",
    )(x, ya, of, ob, ub, yc, yd, on, gbb, wo, gxq, wq, qn, kx, vx, xwo)


def _ffn_kernel(x_ref, g_ref, w13_ref, w2_ref, o_ref):
    x = x_ref[...]
    dff = w2_ref.shape[0]
    a = _dot(_rms(x, g_ref[...]).astype(BF16), w13_ref[...])
    a1 = a[:, :dff]
    o_ref[...] = x + _dot((a1 * _sigmoid(a1) * a[:, dff:]).astype(BF16), w2_ref[...])


def _ffn(x2, g, w13, w2, tm):
    t, d = x2.shape
    single = dict(pipeline_mode=pl.Buffered(1))
    return pl.pallas_call(
        _ffn_kernel,
        grid=(t // tm,),
        in_specs=[pl.BlockSpec((tm, d), lambda i: (i, 0)), _full((1, d)),
                  pl.BlockSpec(w13.shape, lambda i: (0, 0), **single),
                  pl.BlockSpec(w2.shape, lambda i: (0, 0), **single)],
        out_specs=pl.BlockSpec((tm, d), lambda i: (i, 0)),
        out_shape=jax.ShapeDtypeStruct((t, d), F32),
        compiler_params=_params("parallel"),
        name="ffn",
    )(x2, g, w13, w2)


def _pad_heads(w, width):
    lead = w.shape[:-1]
    w = w.reshape(lead + (HEADS, width))
    w = jnp.pad(w, [(0, 0)] * len(lead) + [(0, 0), (0, HPAD - width)])
    return w.reshape(lead + (HEADS * HPAD,))


def _swap_rope(w):
    half = ROPE // 2
    z = jnp.zeros_like(w[..., :HDIM])
    return jnp.concatenate([z, w[..., HDIM + half:], w[..., HDIM:HDIM + half]], axis=-1)


def _mla_weights(wuq, wukv, qn_g, kn_g):
    dq = wuq.shape[0]
    dkv = wukv.shape[0]
    wq3 = wuq.reshape(dq, HEADS, QK)
    wq = jnp.concatenate([_pad_heads(wuq, QK), _pad_heads(_swap_rope(wq3).reshape(dq, -1), QK)], axis=1)
    kv3 = wukv.reshape(dkv, HEADS, 2 * HDIM)
    k_nope = jnp.pad(kv3[..., :HDIM], [(0, 0), (0, 0), (0, HPAD - HDIM)]).reshape(dkv, -1)
    v_cols = kv3[..., HDIM:].reshape(dkv, -1)
    top = jnp.concatenate([k_nope, jnp.zeros_like(k_nope), v_cols], axis=1)
    eye = jnp.eye(ROPE, dtype=F32)
    place = jnp.pad(eye, [(0, 0), (HDIM, HPAD - QK)])
    place_sw = jnp.pad(jnp.roll(eye, ROPE // 2, axis=1), [(0, 0), (HDIM, HPAD - QK)])
    rope_rows = jnp.concatenate([jnp.tile(place, (1, HEADS)), jnp.tile(place_sw, (1, HEADS)),
                                 jnp.zeros((ROPE, v_cols.shape[1]), F32)], axis=1)
    rope_rows = jnp.pad(rope_rows, [(0, HPAD - ROPE), (0, 0)])
    wk = jnp.concatenate([top, rope_rows, rope_rows], axis=0)
    pad1 = lambda g: jnp.pad(g, (0, HPAD - QK))[None, :]
    return (wq.astype(BF16), wk.astype(BF16),
            pad1(qn_g), pad1(_swap_rope(qn_g)), pad1(kn_g), pad1(_swap_rope(kn_g)))


def kernel(x, mem, positions, g_mix, w_in, a_dw_w, a_dw_b, a_ln_g, a_ln_b, h_gamma, h_onorm_g, c_dw_w, c_dw_b, m_qa_g, m_wuq, m_kva_g, m_wukv, m_qn_g, m_kn_g, g_branch, w_out, g_xq, g_mem, x_wq, x_wkv, x_qn_g, x_kn_g, x_wo, g_ffn, f_w13, f_w2):
    b, s, d = x.shape
    t = b * s
    depth = w_in.shape[0]
    row = lambda v: v[None, :]
    tile4 = lambda v: jnp.tile(v, HEADS)[None, :]

    half = ROPE // 2
    inv = ROPE_BASE ** (-jnp.arange(0, ROPE, 2, dtype=F32) / ROPE)
    inv_row = jnp.concatenate([jnp.zeros((HDIM,), F32), inv, inv, jnp.zeros((HPAD - QK,), F32)])[None, :]
    ct, sg = _rope_tables(positions.astype(F32).reshape(t, 1), inv_row, 512)

    n_mem = mem.shape[1]
    kv4 = x_wkv.reshape(depth, d, HEADS, 2, HDIM).transpose(0, 1, 3, 2, 4).reshape(depth, d, 2 * GROUP)
    kx, vx = _mem_kv(mem, g_mem[:, None, :], kv4.astype(BF16), jnp.tile(x_kn_g, (1, HEADS))[:, None, :])

    widths = (2 * GROUP, 5 * GROUP, 3 * GROUP, 2 * GROUP)
    n_in = w_in.shape[2]
    for l in range(depth):
        w_in_l = jnp.pad(w_in[l], [(0, 0), (0, sum(widths) - n_in)]).astype(BF16)
        ua, ub, uc, ud = _in_proj(x.reshape(t, d), row(g_mix[l]), w_in_l, widths, 512)
        ua = ua.reshape(b, s, -1)
        ub = ub.reshape(b, s, -1)
        uc = uc.reshape(b, s, -1)
        gb = g_branch[l]
        wq, wk, gq, gqs, gk, gks = _mla_weights(m_wuq[l], m_wukv[l], m_qn_g[l], m_kn_g[l])
        qr, kr, vr = _mla_prep(ud, ct, sg, row(m_qa_g[l]), row(m_kva_g[l]), wq, wk, gq, gqs, gk, gks, 512)
        of, ob, ya, yc, yd = _mixers(
            ua, ub, uc, qr.reshape(b, s, -1), kr.reshape(b, s, -1), vr.reshape(HEADS, b, s, HPAD), h_gamma,
            a_dw_w[l], row(a_dw_b[l]), row(a_ln_g[l]), row(a_ln_b[l]), c_dw_w[l], row(c_dw_b[l]),
            row(gb[:GROUP]), row(gb[2 * GROUP:3 * GROUP]), row(gb[3 * GROUP:]), l, 512)
        x = _mix_out(x, ya, of, ob, ub, yc, yd, row(h_onorm_g[l]), row(gb[GROUP:2 * GROUP]),
                     w_out[l].astype(BF16), row(g_xq[l]), x_wq[l].astype(BF16), tile4(x_qn_g[l]),
                     kx, vx, x_wo[l].astype(BF16), l, 512)
        x = _ffn(x.reshape(t, d), row(g_ffn[l]), f_w13[l].astype(BF16), f_w2[l].astype(BF16), 256).reshape(b, s, d)
    return x
```

```python
import functools

import jax
import jax.numpy as jnp
from jax import lax
from jax.experimental import pallas as pl
from jax.experimental.pallas import tpu as pltpu

F32 = jnp.float32
BF16 = jnp.bfloat16
EPS = 1e-6

GROUP = 256
HEADS = 4
HDIM = 64
ROPE = 32
QK = HDIM + ROPE
HPAD = 128
CONF_W = 31
SC_W = 3
HALO = 16
STEP = 16
SUBLANES = 8
ROPE_BASE = 10000.0
LOG2E = 1.4426950408889634
VMEM_LIMIT = 60 * 1024 * 1024


def _rms(x, g):
    return x * lax.rsqrt(jnp.mean(x * x, axis=-1, keepdims=True) + EPS) * g


def _sigmoid(x):
    return 1.0 / (1.0 + jnp.exp(-x))


def _dot(a, b):
    return jnp.dot(a, b, preferred_element_type=F32)


def _dot_nt(a, b):
    return lax.dot_general(a, b, (((1,), (1,)), ((), ())), preferred_element_type=F32)


def _dot_tn(a, b):
    return lax.dot_general(a, b, (((0,), (0,)), ((), ())), preferred_element_type=F32)


def _split_dot(m, x):
    hi = x.astype(BF16)
    lo = (x - hi.astype(F32)).astype(BF16)
    return _dot(m, hi) + _dot(m, lo)


def _head_ones(n, width):
    r = lax.broadcasted_iota(jnp.int32, (n, n), 0) // width
    c = lax.broadcasted_iota(jnp.int32, (n, n), 1) // width
    return r == c


def _head_rsqrt(x, ones_bd, width):
    return lax.rsqrt(_split_dot_rhs(x * x, ones_bd) * (1.0 / width) + EPS)


def _split_dot_rhs(x, m):
    hi = x.astype(BF16)
    lo = (x - hi.astype(F32)).astype(BF16)
    return _dot(hi, m) + _dot(lo, m)


def _params(*sem):
    return pltpu.CompilerParams(dimension_semantics=sem, vmem_limit_bytes=VMEM_LIMIT)


def _full(shape):
    nd = len(shape)
    return pl.BlockSpec(shape, lambda *_: (0,) * nd)


def _rope_kernel(pos_ref, inv_ref, ct_ref, sg_ref):
    ang = pos_ref[...] * inv_ref[...]
    lane = lax.broadcasted_iota(jnp.int32, ang.shape, 1)
    c = jnp.cos(ang)
    s = jnp.sin(ang)
    half = ROPE // 2
    ct_ref[...] = jnp.where(lane < HDIM, 1.0, jnp.where(lane < QK, c, 0.0))
    sg_ref[...] = jnp.where(lane < HDIM, 0.0,
                            jnp.where(lane < HDIM + half, -s, jnp.where(lane < QK, s, 0.0)))


def _rope_tables(pos_f, inv_row, tm):
    t = pos_f.shape[0]
    return pl.pallas_call(
        _rope_kernel,
        grid=(t // tm,),
        in_specs=[pl.BlockSpec((tm, 1), lambda i: (i, 0)), _full((1, HPAD))],
        out_specs=[pl.BlockSpec((tm, HPAD), lambda i: (i, 0))] * 2,
        out_shape=[jax.ShapeDtypeStruct((t, HPAD), F32)] * 2,
        compiler_params=_params("parallel"),
        name="rope_tables",
    )(pos_f, inv_row)


def _memkv_kernel(mem_ref, g_ref, w_ref, kn_ref, k_ref, v_ref):
    mn = _rms(mem_ref[0], g_ref[0]).astype(BF16)
    kv = _dot(mn, w_ref[0])
    k = kv[:, :GROUP]
    ones_bd = jnp.where(_head_ones(GROUP, HDIM), 1.0, 0.0).astype(BF16)
    k_ref[0, 0] = (k * _head_rsqrt(k, ones_bd, HDIM) * kn_ref[0]).astype(BF16)
    v_ref[0, 0] = kv[:, GROUP:].astype(BF16)


def _mem_kv(mem, g_mem, wkv, kn_g):
    b, n, d = mem.shape
    nl = wkv.shape[0]
    return pl.pallas_call(
        _memkv_kernel,
        grid=(nl, b),
        in_specs=[pl.BlockSpec((1, n, d), lambda l, i: (i, 0, 0)),
                  pl.BlockSpec((1, 1, d), lambda l, i: (l, 0, 0)),
                  pl.BlockSpec((1, d, 2 * GROUP), lambda l, i: (l, 0, 0)),
                  pl.BlockSpec((1, 1, GROUP), lambda l, i: (l, 0, 0))],
        out_specs=[pl.BlockSpec((1, 1, n, GROUP), lambda l, i: (l, i, 0, 0))] * 2,
        out_shape=[jax.ShapeDtypeStruct((nl, b, n, GROUP), BF16)] * 2,
        compiler_params=_params("parallel", "parallel"),
        name="mem_kv",
    )(mem, g_mem, wkv, kn_g)


def _inproj_kernel(x_ref, g_ref, w_ref, ct_ref, sg_ref, qag_ref, kvag_ref, wq_ref, wk_ref,
                   gq_ref, gqs_ref, gk_ref, gks_ref, ua_ref, ub_ref, uc_ref, q_ref, k_ref, v_ref, *, d_width):
    n = _rms(x_ref[...], g_ref[...]).astype(BF16)
    c = sum(ref.shape[1] for ref in (ua_ref, ub_ref, uc_ref))
    _mla_qkv(_dot(n, w_ref[0, :, c:c + d_width]), ct_ref, sg_ref, qag_ref, kvag_ref, wq_ref, wk_ref,
             gq_ref, gqs_ref, gk_ref, gks_ref, q_ref, k_ref, v_ref)
    c = 0
    for ref in (ua_ref, ub_ref, uc_ref):
        w = ref.shape[1]
        ref[...] = _dot(n, w_ref[0, :, c:c + w])
        c += w


def _in_proj(x2, g, w, ct, sg, mla, widths, layer, tm):
    t, d = x2.shape
    nw = HEADS * HPAD
    row = lambda n: pl.BlockSpec((tm, n), lambda i: (i, 0))
    return pl.pallas_call(
        functools.partial(_inproj_kernel, d_width=widths[3]),
        grid=(t // tm,),
        in_specs=[row(d), _full((1, d)),
                  pl.BlockSpec((1,) + w.shape[1:], lambda i: (layer, 0, 0), pipeline_mode=pl.Buffered(1)),
                  row(HPAD), row(HPAD)] + [_full(a.shape) for a in mla],
        out_specs=[row(n) for n in widths[:3]] + [row(nw), row(nw), pl.BlockSpec((HEADS, tm, HPAD), lambda i: (0, i, 0))],
        out_shape=[jax.ShapeDtypeStruct((t, n), F32) for n in widths[:3]]
        + [jax.ShapeDtypeStruct((t, nw), BF16), jax.ShapeDtypeStruct((t, nw), BF16),
           jax.ShapeDtypeStruct((HEADS, t, HPAD), BF16)],
        compiler_params=_params("parallel"),
        name="in_proj",
    )(x2, g, w, ct, sg, *mla)


def _conv_parts(ua_ref, uap_ref, uan_ref, uc_ref, ucp_ref, ucn_ref,
                aw_ref, ab_ref, lng_ref, lnb_ref, cw_ref, cb_ref, gba_ref, gbc_ref,
                ya_ref, yc_ref, ha_scr, hc_scr, sub):
    tc = ua_ref.shape[1]
    pad_a = (CONF_W - 1) // 2
    pad_c = (SC_W - 1) // 2

    def glu(u):
        return u[:, :GROUP] * _sigmoid(u[:, GROUP:])

    def gated(u):
        return u[:, GROUP:2 * GROUP] * u[:, 2 * GROUP:]

    def fill():
        i = pl.program_id(1)
        has_prev = jnp.where(i > 0, 1.0, 0.0)
        has_next = jnp.where(i < pl.num_programs(1) - 1, 1.0, 0.0)
        ha_scr[0:HALO, :] = glu(uap_ref[0]) * has_prev
        ha_scr[HALO:HALO + tc, :] = glu(ua_ref[0])
        ha_scr[HALO + tc:, :] = glu(uan_ref[0]) * has_next
        hc_scr[0:HALO, :] = gated(ucp_ref[0]) * has_prev
        hc_scr[HALO:HALO + tc, :] = gated(uc_ref[0])
        hc_scr[HALO + tc:, :] = gated(ucn_ref[0]) * has_next

    def tile(r0):
        acc = jnp.broadcast_to(ab_ref[...], (sub, GROUP))
        for r in range(SUBLANES):
            part = None
            for j in range(CONF_W):
                start = HALO - pad_a + j
                if start % SUBLANES != r:
                    continue
                lo = r0 + start - r
                term = aw_ref[j:j + 1, :] * ha_scr[lo:lo + sub + SUBLANES, :]
                part = term if part is None else part + term
            if part is not None:
                acc = acc + part[r:r + sub, :]
        mu = jnp.mean(acc, axis=-1, keepdims=True)
        xc = acc - mu
        y = xc * lax.rsqrt(jnp.mean(xc * xc, axis=-1, keepdims=True) + EPS) * lng_ref[...] + lnb_ref[...]
        y = y * _sigmoid(y)
        ya_ref[0, r0:r0 + sub, :] = _rms(y, gba_ref[...]).astype(BF16)

        acc = jnp.broadcast_to(cb_ref[...], (sub, GROUP))
        for j in range(SC_W):
            acc = acc + cw_ref[j:j + 1, :] * hc_scr[r0 + HALO - pad_c + j:r0 + HALO - pad_c + j + sub, :]
        y = uc_ref[0, r0:r0 + sub, 0:GROUP] * acc
        yc_ref[0, r0:r0 + sub, :] = _rms(y, gbc_ref[...]).astype(BF16)

    return fill, tile


def _hgrn_parts(gam_ref, qf_ref, zf_ref, vf_ref, qb_ref, zb_ref, vb_ref, of_ref, ob_ref,
                st_f, st_b, b_f, b_b, c_f, c_b, lam_f, lam_b, qe_f, qe_b, kt_f, kt_b,
                sts_f, sts_b, ones_scr, layer):
    i = pl.program_id(1)
    tb = qf_ref.shape[1]
    nst = tb // STEP
    half = STEP // 2
    cs = 256

    @pl.when(i == 0)
    def _():
        st_f[...] = jnp.zeros_like(st_f)
        st_b[...] = jnp.zeros_like(st_b)

    ones_scr[...] = jnp.where(_head_ones(GROUP, HDIM), 1.0, 0.0).astype(BF16)

    ti = lax.broadcasted_iota(jnp.int32, (cs, cs), 0)
    si = lax.broadcasted_iota(jnp.int32, (cs, cs), 1)
    same = (ti // STEP) == (si // STEP)
    m_all = jnp.where(same, 1.0, 0.0).astype(BF16)

    def prep(d, q_ref, z_ref, b_scr, c_scr, lam_scr, qe_scr, kt_scr):
        g = gam_ref[d]
        e = jnp.exp(g - jnp.max(g, axis=0, keepdims=True))
        tot = jnp.sum(e, axis=0, keepdims=True)
        if layer == 0:
            lb = jnp.zeros_like(tot)
        else:
            lb = jnp.sum(e[1:layer + 1], axis=0, keepdims=True) / tot
        tri = (si <= ti) if d == 0 else (si >= ti)
        m_tri = jnp.where(same & tri, 1.0, 0.0).astype(BF16)
        for r0 in range(0, tb, cs):
            z = z_ref[0, r0:r0 + cs, :]
            a = jnp.exp(-jnp.abs(z))
            big = 1.0 / (1.0 + a)
            pos = z >= 0.0
            f = lb + (1.0 - lb) * jnp.where(pos, big, a * big)
            kk = (1.0 - lb) * jnp.where(pos, a * big, big)
            lf = jnp.log2(f)
            bloc = _split_dot(m_tri, lf)
            btot = _split_dot(m_all, lf)
            b_scr[r0:r0 + cs, :] = bloc
            c_scr[r0:r0 + cs, :] = bloc - jnp.log2(kk)
            lam_scr[r0:r0 + cs, :] = jnp.exp2(btot)
            qe_scr[r0:r0 + cs, :] = (q_ref[0, r0:r0 + cs, :] * jnp.exp2(bloc)).astype(BF16)
            kt_scr[r0:r0 + cs, :] = (kk * jnp.exp2(btot - bloc)).astype(BF16)

    prep(0, qf_ref, zf_ref, b_f, c_f, lam_f, qe_f, kt_f)
    prep(1, qb_ref, zb_ref, b_b, c_b, lam_b, qe_b, kt_b)

    trow = lax.broadcasted_iota(jnp.int32, (half, GROUP), 0)
    head_of_lane = lax.broadcasted_iota(jnp.int32, (STEP, GROUP), 1) // HDIM

    def liveness(s, rev):
        if rev:
            return (1, 0) if s < half else (2, 1)
        return (1, 2) if s < half else (0, 1)

    def per_head_rows(x):
        zero = jnp.zeros_like(x)
        return jnp.concatenate([jnp.where(head_of_lane == h, x, zero) for h in range(HEADS)], axis=0)

    def advance(k, r, state, v_ref, kt_scr, lam_scr, sts_scr):
        vb = v_ref[0, pl.ds(r, STEP), :].astype(BF16)
        vstack = jnp.concatenate([vb[:, h * HDIM:(h + 1) * HDIM] for h in range(HEADS)], axis=0)
        upd = _dot_tn(vstack, per_head_rows(kt_scr[pl.ds(r, STEP), :]))
        sts_scr[pl.ds(k * HDIM, HDIM), :] = state.astype(BF16)
        return lam_scr[pl.ds(r, 1), :] * state + upd

    def output(k, r, rev, q_ref, v_ref, o_ref, b_scr, c_scr, qe_scr, sts_scr):
        res = _dot_nt(per_head_rows(qe_scr[pl.ds(r, STEP), :]), sts_scr[pl.ds(k * HDIM, HDIM), :])
        o = jnp.concatenate([res[h * STEP:(h + 1) * STEP, :] for h in range(HEADS)], axis=1)
        o = [o[:half], o[half:]]
        q = (q_ref[0, pl.ds(r, half), :], q_ref[0, pl.ds(r + half, half), :])
        b = (b_scr[pl.ds(r, half), :], b_scr[pl.ds(r + half, half), :])
        pieces, where_to = [], []
        for s in range(STEP):
            crow = jnp.broadcast_to(c_scr[pl.ds(r + s, 1), :], (half, GROUP))
            for hx, kind in enumerate(liveness(s, rev)):
                if kind == 0:
                    continue
                dec = q[hx] * jnp.exp2(b[hx] - crow)
                if kind == 1:
                    sl = s - hx * half
                    dec = jnp.where((trow <= sl) if rev else (trow >= sl), dec, 0.0)
                pieces.append(dec)
                where_to.append((s, hx))
        att = _dot(jnp.concatenate(pieces, axis=0).astype(BF16), ones_scr[...])
        vrows = [jnp.broadcast_to(v_ref[0, pl.ds(r + s, 1), :], (half, GROUP)) for s in range(STEP)]
        for n, (s, hx) in enumerate(where_to):
            o[hx] = o[hx] + att[n * half:(n + 1) * half, :] * vrows[s]
        o_ref[0, pl.ds(r, half), :] = o[0]
        o_ref[0, pl.ds(r + half, half), :] = o[1]

    states = [st_f[...], st_b[...]]

    def state_step(k):
        states[0] = advance(k, k * STEP, states[0], vf_ref, kt_f, lam_f, sts_f)
        states[1] = advance(k, (nst - 1 - k) * STEP, states[1], vb_ref, kt_b, lam_b, sts_b)
        if k == nst - 1:
            st_f[...] = states[0]
            st_b[...] = states[1]

    def output_step(k):
        output(k, k * STEP, False, qf_ref, vf_ref, of_ref, b_f, c_f, qe_f, sts_f)
        output(k, (nst - 1 - k) * STEP, True, qb_ref, vb_ref, ob_ref, b_b, c_b, qe_b, sts_b)

    return nst, state_step, output_step


def _hgrn_scratch(tb):
    vm = lambda dt: pltpu.VMEM((tb, GROUP), dt)
    return ([pltpu.VMEM((HDIM, GROUP), F32)] * 2 + [vm(F32)] * 6 + [vm(BF16)] * 4
            + [pltpu.VMEM((tb // STEP * HDIM, GROUP), BF16)] * 2
            + [pltpu.VMEM((GROUP, GROUP), BF16)])


def _mla_qkv(ud, ct_ref, sg_ref, qag_ref, kvag_ref, wq_ref, wk_ref,
             gq_ref, gqs_ref, gk_ref, gks_ref, q_ref, k_ref, v_ref):
    nw = HEADS * HPAD
    nq = _rms(ud[:, :GROUP], qag_ref[...]).astype(BF16)
    qq = _dot(nq, wq_ref[...])
    nkv = _rms(ud[:, GROUP:GROUP + HPAD], kvag_ref[...]).astype(BF16)
    kr = ud[:, GROUP + HPAD:]
    kr_hi = kr.astype(BF16)
    kr_lo = (kr - kr_hi.astype(F32)).astype(BF16)
    kk = _dot(jnp.concatenate([nkv, kr_hi, kr_lo], axis=-1), wk_ref[...])
    ct = ct_ref[...]
    sg = sg_ref[...]
    scale = QK ** -0.5 * LOG2E
    for h in range(HEADS):
        sl = slice(h * HPAD, (h + 1) * HPAD)
        sw = slice(nw + h * HPAD, nw + (h + 1) * HPAD)
        x = qq[:, sl]
        r = lax.rsqrt(jnp.sum(x * x, axis=-1, keepdims=True) * (1.0 / QK) + EPS) * scale
        q_ref[:, sl] = (r * (x * (gq_ref[...] * ct) + qq[:, sw] * (gqs_ref[...] * sg))).astype(BF16)
        x = kk[:, sl]
        r = lax.rsqrt(jnp.sum(x * x, axis=-1, keepdims=True) * (1.0 / QK) + EPS)
        k_ref[:, sl] = (r * (x * (gk_ref[...] * ct) + kk[:, sw] * (gks_ref[...] * sg))).astype(BF16)
    first = lax.broadcasted_iota(jnp.int32, (ud.shape[0], HPAD), 1) < HDIM
    for pair in range(HEADS // 2):
        vv = kk[:, 2 * nw + pair * HPAD:2 * nw + (pair + 1) * HPAD]
        v_ref[2 * pair] = jnp.where(first, vv, 1.0).astype(BF16)
        v_ref[2 * pair + 1] = jnp.where(first, 1.0, vv).astype(BF16)


def _attn_parts(q_ref, k_ref, v_ref, gb_ref, o_ref, sb_scr, rows, ck):
    tq = q_ref.shape[1]
    n_keys = k_ref.shape[1]
    first = lax.broadcasted_iota(jnp.int32, (rows, HPAD), 1) < HDIM
    items = [(r0, h) for r0 in range(0, tq, rows) for h in range(HEADS)]
    done = []

    def scores(n):
        r0, h = items[n]
        sl = slice(h * HPAD, (h + 1) * HPAD)
        q = q_ref[0, r0:r0 + rows, sl]
        top = None
        for c in range(0, n_keys, ck):
            sb = _dot_nt(q, k_ref[0, c:c + ck, sl]).astype(BF16)
            sb_scr[n % 2, :, c:c + ck] = sb
            for j in range(0, ck, HPAD):
                top = sb[:, j:j + HPAD] if top is None else jnp.maximum(top, sb[:, j:j + HPAD])
        return jnp.max(top, axis=-1, keepdims=True)

    def finish(n, m):
        r0, h = items[n]
        w = None
        for c in range(0, n_keys, ck):
            p = jnp.exp2(sb_scr[n % 2, :, c:c + ck] - m)
            part = _dot(p, v_ref[h, 0, c:c + ck, :])
            w = part if w is None else w + part
        w = w / pltpu.roll(w, HDIM, axis=1)
        done.append(w)
        if h == HEADS - 1:
            pairs = [jnp.where(first, done[2 * j], done[2 * j + 1]) for j in range(HEADS // 2)]
            del done[:]
            o_ref[0, r0:r0 + rows, :] = _rms(jnp.concatenate(pairs, axis=1), gb_ref[...]).astype(BF16)

    return items, scores, finish


N_HGRN_SCRATCH = 15


def _mixers_kernel(*refs, layer, rows, key_chunk, conv_rows):
    (gam_ref, qf_ref, zf_ref, vf_ref, qb_ref, zb_ref, vb_ref,
     ua_ref, uap_ref, uan_ref, uc_ref, ucp_ref, ucn_ref,
     aw_ref, ab_ref, lng_ref, lnb_ref, cw_ref, cb_ref, gba_ref, gbc_ref,
     q_ref, k_ref, v_ref, gbd_ref,
     of_ref, ob_ref, ya_ref, yc_ref, yd_ref) = refs[:30]
    hgrn_scr = refs[30:30 + N_HGRN_SCRATCH]
    ha_scr, hc_scr, sb_scr = refs[30 + N_HGRN_SCRATCH:]

    conv_fill, conv_tile = _conv_parts(ua_ref, uap_ref, uan_ref, uc_ref, ucp_ref, ucn_ref,
                                       aw_ref, ab_ref, lng_ref, lnb_ref, cw_ref, cb_ref, gba_ref, gbc_ref,
                                       ya_ref, yc_ref, ha_scr, hc_scr, conv_rows)
    conv_fill()
    nst, state_step, output_step = _hgrn_parts(gam_ref, qf_ref, zf_ref, vf_ref, qb_ref, zb_ref, vb_ref,
                                               of_ref, ob_ref, *hgrn_scr, layer)
    items, scores, finish = _attn_parts(q_ref, k_ref, v_ref, gbd_ref, yd_ref, sb_scr, rows, key_chunk)
    conv_tiles = list(range(0, ua_ref.shape[1], conv_rows))

    def share(seq, n, first, count):
        n -= first
        return seq[n * len(seq) // count:(n + 1) * len(seq) // count] if 0 <= n < count else []

    early = len(items) // 4
    m_next = scores(0)
    for n in range(len(items)):
        m = m_next
        if n + 1 < len(items):
            m_next = scores(n + 1)
        finish(n, m)
        for k in share(range(nst), n, 0, early):
            state_step(k)
        for k in share(range(nst), n, early, len(items) - early):
            output_step(k)
        for r0 in share(conv_tiles, n, 0, len(items)):
            conv_tile(r0)


def _mixers(ua, ub, uc, q, k, v, gamma, aw, ab, lng, lnb, cw, cb, gba, gbc, gbd, layer, tile):
    b, s, _ = ub.shape
    nb = s // tile
    hb = tile // HALO
    last = s // HALO - 1
    col = lambda j, rev: pl.BlockSpec(
        (1, tile, GROUP), (lambda bi, i: (bi, nb - 1 - i, j)) if rev else (lambda bi, i: (bi, i, j)))
    main = lambda w: pl.BlockSpec((1, tile, w), lambda bi, i: (bi, i, 0))
    prev = lambda w: pl.BlockSpec((1, HALO, w), lambda bi, i: (bi, jnp.maximum(i * hb - 1, 0), 0))
    nxt = lambda w: pl.BlockSpec((1, HALO, w), lambda bi, i: (bi, jnp.minimum((i + 1) * hb, last), 0))
    whole = lambda w: pl.BlockSpec((1, s, w), lambda bi, i: (bi, 0, 0), pipeline_mode=pl.Buffered(1))
    wa, wc, nw = ua.shape[2], uc.shape[2], q.shape[2]
    small = [aw, ab, lng, lnb, cw, cb, gba, gbc]
    return pl.pallas_call(
        functools.partial(_mixers_kernel, layer=layer, rows=256, key_chunk=512, conv_rows=64),
        grid=(b, nb),
        in_specs=[_full(gamma.shape),
                  col(0, False), col(1, False), col(3, False), col(0, True), col(2, True), col(3, True),
                  main(wa), prev(wa), nxt(wa), main(wc), prev(wc), nxt(wc)]
        + [_full(a.shape) for a in small]
        + [main(nw), whole(nw),
           pl.BlockSpec((HEADS, 1, s, HPAD), lambda bi, i: (0, bi, 0, 0), pipeline_mode=pl.Buffered(1)),
           _full(gbd.shape)],
        out_specs=[main(GROUP), pl.BlockSpec((1, tile, GROUP), lambda bi, i: (bi, nb - 1 - i, 0)),
                   main(GROUP), main(GROUP), main(GROUP)],
        out_shape=[jax.ShapeDtypeStruct((b, s, GROUP), F32)] * 2 + [jax.ShapeDtypeStruct((b, s, GROUP), BF16)] * 3,
        scratch_shapes=_hgrn_scratch(tile) + [pltpu.VMEM((tile + 2 * HALO, GROUP), F32)] * 2
        + [pltpu.VMEM((2, 256, s), BF16)],
        compiler_params=_params("parallel", "arbitrary"),
        name="mixers",
    )(gamma, ub, ub, ub, ub, ub, ub, ua, ua, ua, uc, uc, uc, *small, q, k, v, gbd)


def _channel_kernel(x_ref, ya_ref, of_ref, ob_ref, g_ref, yc_ref, yd_ref, on_ref, gbb_ref, wo_ref,
                    gxq_ref, wq_ref, qn_ref, kx_ref, vx_ref, xwo_ref, gf_ref, w13_ref, w2_ref, o_ref, *, rows):
    tm = x_ref.shape[1]
    dff = w2_ref.shape[1]
    ones_bd = jnp.where(_head_ones(GROUP, HDIM), 1.0, 0.0).astype(BF16)
    lane = lax.broadcasted_iota(jnp.int32, (rows, GROUP), 1) // HDIM

    def mix(r0):
        sl = slice(r0, r0 + rows)
        o = of_ref[0, sl, :] + ob_ref[0, sl, :]
        g = g_ref[0, sl, :]
        yb = o * _head_rsqrt(o, ones_bd, HDIM) * on_ref[...] * (g * _sigmoid(g))
        yb = _rms(yb, gbb_ref[...]).astype(BF16)
        x1 = x_ref[0, sl, :]
        for j, y in enumerate((ya_ref[0, sl, :], yb, yc_ref[0, sl, :], yd_ref[0, sl, :])):
            x1 = x1 + _dot(y, wo_ref[0, j * GROUP:(j + 1) * GROUP, :])
        q = _dot(_rms(x1, gxq_ref[...]).astype(BF16), wq_ref[0])
        qn = q * _head_rsqrt(q, ones_bd, HDIM) * (qn_ref[...] * HDIM ** -0.5)
        acc = jnp.zeros((rows, GROUP), F32)
        for h in range(HEADS):
            s = _dot_nt(jnp.where(lane == h, qn, 0.0).astype(BF16), kx_ref[0, 0])
            p = jnp.exp(s - jnp.max(s, axis=-1, keepdims=True))
            l = jnp.sum(p, axis=-1, keepdims=True)
            pv = _dot(p.astype(BF16), vx_ref[0, 0])
            acc = jnp.where(lane == h, pv / l, acc)
        return x1 + _dot(acc.astype(BF16), xwo_ref[0])

    def ffn(x):
        a = _dot(_rms(x, gf_ref[...]).astype(BF16), w13_ref[0])
        a1 = a[:, :dff]
        return x + _dot((a1 * _sigmoid(a1) * a[:, dff:]).astype(BF16), w2_ref[0])

    starts = range(0, tm, rows)
    mixed = [mix(r0) for r0 in starts]
    for r0, x in zip(starts, mixed):
        o_ref[0, r0:r0 + rows, :] = ffn(x)


def _channel(x, ya, of, ob, ub, yc, yd, on, gbb, wo, gxq, wq, qn, kx, vx, xwo, gf, w13, w2, layer, tm, rows):
    b, s, d = x.shape
    n = kx.shape[2]
    row = lambda w, j=0: pl.BlockSpec((1, tm, w), lambda bi, i: (bi, i, j))
    of_layer = lambda a: pl.BlockSpec((1,) + a.shape[1:], lambda bi, i: (layer,) + (0,) * (a.ndim - 1),
                                      pipeline_mode=pl.Buffered(1))
    return pl.pallas_call(
        functools.partial(_channel_kernel, rows=rows),
        grid=(b, s // tm),
        in_specs=[row(d), row(GROUP), row(GROUP), row(GROUP), row(GROUP, 4), row(GROUP), row(GROUP),
                  _full(on.shape), _full(gbb.shape), of_layer(wo), _full(gxq.shape), of_layer(wq),
                  _full(qn.shape),
                  pl.BlockSpec((1, 1, n, GROUP), lambda bi, i: (layer, bi, 0, 0)),
                  pl.BlockSpec((1, 1, n, GROUP), lambda bi, i: (layer, bi, 0, 0)),
                  of_layer(xwo), _full(gf.shape), of_layer(w13), of_layer(w2)],
        out_specs=row(d),
        out_shape=jax.ShapeDtypeStruct((b, s, d), F32),
        compiler_params=_params("parallel", "parallel"),
        name="channel",
    )(x, ya, of, ob, ub, yc, yd, on, gbb, wo, gxq, wq, qn, kx, vx, xwo, gf, w13, w2)


def _pad_heads(w, width):
    lead = w.shape[:-1]
    w = w.reshape(lead + (HEADS, width))
    w = jnp.pad(w, [(0, 0)] * len(lead) + [(0, 0), (0, HPAD - width)])
    return w.reshape(lead + (HEADS * HPAD,))


def _swap_rope(w):
    half = ROPE // 2
    z = jnp.zeros_like(w[..., :HDIM])
    return jnp.concatenate([z, w[..., HDIM + half:], w[..., HDIM:HDIM + half]], axis=-1)


def _mla_weights(wuq, wukv, qn_g, kn_g):
    dq = wuq.shape[0]
    dkv = wukv.shape[0]
    wq3 = wuq.reshape(dq, HEADS, QK)
    wq = jnp.concatenate([_pad_heads(wuq, QK), _pad_heads(_swap_rope(wq3).reshape(dq, -1), QK)], axis=1)
    kv3 = wukv.reshape(dkv, HEADS, 2 * HDIM)
    k_nope = jnp.pad(kv3[..., :HDIM], [(0, 0), (0, 0), (0, HPAD - HDIM)]).reshape(dkv, -1)
    v_cols = kv3[..., HDIM:].reshape(dkv, -1)
    top = jnp.concatenate([k_nope, jnp.zeros_like(k_nope), v_cols], axis=1)
    eye = jnp.eye(ROPE, dtype=F32)
    place = jnp.pad(eye, [(0, 0), (HDIM, HPAD - QK)])
    place_sw = jnp.pad(jnp.roll(eye, ROPE // 2, axis=1), [(0, 0), (HDIM, HPAD - QK)])
    rope_rows = jnp.concatenate([jnp.tile(place, (1, HEADS)), jnp.tile(place_sw, (1, HEADS)),
                                 jnp.zeros((ROPE, v_cols.shape[1]), F32)], axis=1)
    rope_rows = jnp.pad(rope_rows, [(0, HPAD - ROPE), (0, 0)])
    wk = jnp.concatenate([top, rope_rows, rope_rows], axis=0)
    pad1 = lambda g: jnp.pad(g, (0, HPAD - QK))[None, :]
    return (wq.astype(BF16), wk.astype(BF16),
            pad1(qn_g), pad1(_swap_rope(qn_g)), pad1(kn_g), pad1(_swap_rope(kn_g)))


def kernel(x, mem, positions, g_mix, w_in, a_dw_w, a_dw_b, a_ln_g, a_ln_b, h_gamma, h_onorm_g, c_dw_w, c_dw_b, m_qa_g, m_wuq, m_kva_g, m_wukv, m_qn_g, m_kn_g, g_branch, w_out, g_xq, g_mem, x_wq, x_wkv, x_qn_g, x_kn_g, x_wo, g_ffn, f_w13, f_w2):
    b, s, d = x.shape
    t = b * s
    depth = w_in.shape[0]
    row = lambda v: v[None, :]
    tile4 = lambda v: jnp.tile(v, HEADS)[None, :]

    half = ROPE // 2
    inv = ROPE_BASE ** (-jnp.arange(0, ROPE, 2, dtype=F32) / ROPE)
    inv_row = jnp.concatenate([jnp.zeros((HDIM,), F32), inv, inv, jnp.zeros((HPAD - QK,), F32)])[None, :]
    ct, sg = _rope_tables(positions.astype(F32).reshape(t, 1), inv_row, 512)

    n_mem = mem.shape[1]
    kv4 = x_wkv.reshape(depth, d, HEADS, 2, HDIM).transpose(0, 1, 3, 2, 4).reshape(depth, d, 2 * GROUP)
    kx, vx = _mem_kv(mem, g_mem[:, None, :], kv4.astype(BF16), jnp.tile(x_kn_g, (1, HEADS))[:, None, :])

    wo_all, xwq_all, xwo_all = w_out.astype(BF16), x_wq.astype(BF16), x_wo.astype(BF16)
    w13_all, w2_all = f_w13.astype(BF16), f_w2.astype(BF16)
    widths = (2 * GROUP, 5 * GROUP, 3 * GROUP, 2 * GROUP)
    w_in_all = jnp.pad(w_in, [(0, 0), (0, 0), (0, sum(widths) - w_in.shape[2])]).astype(BF16)
    for l in range(depth):
        mla = (row(m_qa_g[l]), row(m_kva_g[l])) + _mla_weights(m_wuq[l], m_wukv[l], m_qn_g[l], m_kn_g[l])
        ua, ub, uc, qr, kr, vr = _in_proj(x.reshape(t, d), row(g_mix[l]), w_in_all, ct, sg, mla, widths, l, 512)
        ua = ua.reshape(b, s, -1)
        ub = ub.reshape(b, s, -1)
        uc = uc.reshape(b, s, -1)
        gb = g_branch[l]
        of, ob, ya, yc, yd = _mixers(
            ua, ub, uc, qr.reshape(b, s, -1), kr.reshape(b, s, -1), vr.reshape(HEADS, b, s, HPAD), h_gamma,
            a_dw_w[l], row(a_dw_b[l]), row(a_ln_g[l]), row(a_ln_b[l]), c_dw_w[l], row(c_dw_b[l]),
            row(gb[:GROUP]), row(gb[2 * GROUP:3 * GROUP]), row(gb[3 * GROUP:]), l, 512)
        x = _channel(x, ya, of, ob, ub, yc, yd, row(h_onorm_g[l]), row(gb[GROUP:2 * GROUP]), wo_all,
                     row(g_xq[l]), xwq_all, tile4(x_qn_g[l]), kx, vx, xwo_all, row(g_ffn[l]), w13_all, w2_all,
                     l, 512, 256)
    return x
```

```python
import functools

import jax
import jax.numpy as jnp
from jax import lax
from jax.experimental import pallas as pl
from jax.experimental.pallas import tpu as pltpu

F32 = jnp.float32
BF16 = jnp.bfloat16
EPS = 1e-6

GROUP = 256
HEADS = 4
HDIM = 64
ROPE = 32
QK = HDIM + ROPE
HPAD = 128
CONF_W = 31
SC_W = 3
HALO = 16
STEP = 16
SUBLANES = 8
ROPE_BASE = 10000.0
LOG2E = 1.4426950408889634
VMEM_LIMIT = 60 * 1024 * 1024

TOKEN_TILE = 512
SUB_ROWS = 256
KEY_CHUNK = 512
CONV_ROWS = 64


def _rms(x, g):
    return x * lax.rsqrt(jnp.mean(x * x, axis=-1, keepdims=True) + EPS) * g


def _sigmoid(x):
    return 1.0 / (1.0 + jnp.exp(-x))


def _dot(a, b):
    return jnp.dot(a, b, preferred_element_type=F32)


def _dot_nt(a, b):
    return lax.dot_general(a, b, (((1,), (1,)), ((), ())), preferred_element_type=F32)


def _dot_tn(a, b):
    return lax.dot_general(a, b, (((0,), (0,)), ((), ())), preferred_element_type=F32)


def _split_dot(m, x):
    hi = x.astype(BF16)
    lo = (x - hi.astype(F32)).astype(BF16)
    return _dot(m, hi) + _dot(m, lo)


def _head_ones(n, width):
    r = lax.broadcasted_iota(jnp.int32, (n, n), 0) // width
    c = lax.broadcasted_iota(jnp.int32, (n, n), 1) // width
    return r == c


def _head_rsqrt(x, ones_bd, width):
    return lax.rsqrt(_split_dot_rhs(x * x, ones_bd) * (1.0 / width) + EPS)


def _split_dot_rhs(x, m):
    hi = x.astype(BF16)
    lo = (x - hi.astype(F32)).astype(BF16)
    return _dot(hi, m) + _dot(lo, m)


def _params(*sem):
    return pltpu.CompilerParams(dimension_semantics=sem, vmem_limit_bytes=VMEM_LIMIT)


def _full(shape):
    nd = len(shape)
    return pl.BlockSpec(shape, lambda *_: (0,) * nd)


def _rope_kernel(pos_ref, inv_ref, ct_ref, sg_ref):
    ang = pos_ref[...] * inv_ref[...]
    lane = lax.broadcasted_iota(jnp.int32, ang.shape, 1)
    c = jnp.cos(ang)
    s = jnp.sin(ang)
    half = ROPE // 2
    ct_ref[...] = jnp.where(lane < HDIM, 1.0, jnp.where(lane < QK, c, 0.0))
    sg_ref[...] = jnp.where(lane < HDIM, 0.0,
                            jnp.where(lane < HDIM + half, -s, jnp.where(lane < QK, s, 0.0)))


def _rope_tables(pos_f, inv_row, tm):
    t = pos_f.shape[0]
    return pl.pallas_call(
        _rope_kernel,
        grid=(t // tm,),
        in_specs=[pl.BlockSpec((tm, 1), lambda i: (i, 0)), _full((1, HPAD))],
        out_specs=[pl.BlockSpec((tm, HPAD), lambda i: (i, 0))] * 2,
        out_shape=[jax.ShapeDtypeStruct((t, HPAD), F32)] * 2,
        compiler_params=_params("parallel"),
        name="rope_tables",
    )(pos_f, inv_row)


def _memkv_kernel(mem_ref, g_ref, w_ref, kn_ref, k_ref, v_ref):
    mn = _rms(mem_ref[0], g_ref[0]).astype(BF16)
    kv = _dot(mn, w_ref[0])
    k = kv[:, :GROUP]
    ones_bd = jnp.where(_head_ones(GROUP, HDIM), 1.0, 0.0).astype(BF16)
    k_ref[0, 0] = (k * _head_rsqrt(k, ones_bd, HDIM) * kn_ref[0]).astype(BF16)
    v_ref[0, 0] = kv[:, GROUP:].astype(BF16)


def _mem_kv(mem, g_mem, wkv, kn_g):
    b, n, d = mem.shape
    nl = wkv.shape[0]
    return pl.pallas_call(
        _memkv_kernel,
        grid=(nl, b),
        in_specs=[pl.BlockSpec((1, n, d), lambda l, i: (i, 0, 0)),
                  pl.BlockSpec((1, 1, d), lambda l, i: (l, 0, 0)),
                  pl.BlockSpec((1, d, 2 * GROUP), lambda l, i: (l, 0, 0)),
                  pl.BlockSpec((1, 1, GROUP), lambda l, i: (l, 0, 0))],
        out_specs=[pl.BlockSpec((1, 1, n, GROUP), lambda l, i: (l, i, 0, 0))] * 2,
        out_shape=[jax.ShapeDtypeStruct((nl, b, n, GROUP), BF16)] * 2,
        compiler_params=_params("parallel", "parallel"),
        name="mem_kv",
    )(mem, g_mem, wkv, kn_g)


def _inproj_kernel(x_ref, xp_ref, xn_ref, g_ref, w_ref, ct_ref, sg_ref, qag_ref, kvag_ref, wq_ref, wk_ref,
                   gq_ref, gqs_ref, gk_ref, gks_ref, aw_ref, ab_ref, lng_ref, lnb_ref, cw_ref, cb_ref, gba_ref, gbc_ref,
                   ub_ref, q_ref, k_ref, v_ref, ya_ref, yc_ref, ha_scr, hc_scr, gate_scr,
                   *, widths, tiles_per_seq, conv_rows):
    wa, wb, wc, wd = widths
    i = pl.program_id(0) % tiles_per_seq
    norm = lambda x: _rms(x, g_ref[...]).astype(BF16)
    n = norm(x_ref[...])
    _mla_qkv(_dot(n, w_ref[0, :, wa + wb + wc:wa + wb + wc + wd]), ct_ref, sg_ref, qag_ref, kvag_ref, wq_ref, wk_ref,
             gq_ref, gqs_ref, gk_ref, gks_ref, q_ref, k_ref, v_ref)
    n_ext = jnp.concatenate([norm(xp_ref[...]), n, norm(xn_ref[...])], axis=0)
    conv_fill, conv_tile = _conv_parts(aw_ref, ab_ref, lng_ref, lnb_ref, cw_ref, cb_ref, gba_ref, gbc_ref,
                                       ya_ref, yc_ref, ha_scr, hc_scr, gate_scr, conv_rows)
    conv_fill(_dot(n_ext, w_ref[0, :, 0:wa]), _dot(n_ext, w_ref[0, :, wa + wb:wa + wb + wc]),
              jnp.where(i > 0, 1.0, 0.0), jnp.where(i < tiles_per_seq - 1, 1.0, 0.0))
    ub_ref[...] = _dot(n, w_ref[0, :, wa:wa + wb])
    for r0 in range(0, x_ref.shape[0], conv_rows):
        conv_tile(r0)


def _in_proj(x2, g, w, ct, sg, mla, conv, widths, layer, tm, seq):
    t, d = x2.shape
    nw = HEADS * HPAD
    hb = tm // HALO
    last = t // HALO - 1
    row = lambda n: pl.BlockSpec((tm, n), lambda i: (i, 0))
    return pl.pallas_call(
        functools.partial(_inproj_kernel, widths=widths, tiles_per_seq=seq // tm, conv_rows=CONV_ROWS),
        grid=(t // tm,),
        in_specs=[row(d),
                  pl.BlockSpec((HALO, d), lambda i: (jnp.maximum(i * hb - 1, 0), 0)),
                  pl.BlockSpec((HALO, d), lambda i: (jnp.minimum((i + 1) * hb, last), 0)),
                  _full((1, d)),
                  pl.BlockSpec((1,) + w.shape[1:], lambda i: (layer, 0, 0), pipeline_mode=pl.Buffered(1)),
                  row(HPAD), row(HPAD)] + [_full(a.shape) for a in mla + conv],
        out_specs=[row(widths[1]), row(nw), row(nw), pl.BlockSpec((HEADS, tm, HPAD), lambda i: (0, i, 0)),
                   row(GROUP), row(GROUP)],
        out_shape=[jax.ShapeDtypeStruct((t, widths[1]), F32),
                   jax.ShapeDtypeStruct((t, nw), BF16), jax.ShapeDtypeStruct((t, nw), BF16),
                   jax.ShapeDtypeStruct((HEADS, t, HPAD), BF16),
                   jax.ShapeDtypeStruct((t, GROUP), BF16), jax.ShapeDtypeStruct((t, GROUP), BF16)],
        scratch_shapes=[pltpu.VMEM((tm + 2 * HALO, GROUP), F32)] * 2 + [pltpu.VMEM((tm, GROUP), F32)],
        compiler_params=_params("parallel"),
        name="in_proj",
    )(x2, x2, x2, g, w, ct, sg, *mla, *conv)


def _conv_parts(aw_ref, ab_ref, lng_ref, lnb_ref, cw_ref, cb_ref, gba_ref, gbc_ref,
                ya_ref, yc_ref, ha_scr, hc_scr, gate_scr, sub):
    tc = gate_scr.shape[0]
    pad_a = (CONF_W - 1) // 2
    pad_c = (SC_W - 1) // 2

    def fill(ua, uc, has_prev, has_next):
        ha = ua[:, :GROUP] * _sigmoid(ua[:, GROUP:])
        hc = uc[:, GROUP:2 * GROUP] * uc[:, 2 * GROUP:]
        for scr, h in ((ha_scr, ha), (hc_scr, hc)):
            scr[0:HALO, :] = h[0:HALO] * has_prev
            scr[HALO:HALO + tc, :] = h[HALO:HALO + tc]
            scr[HALO + tc:, :] = h[HALO + tc:] * has_next
        gate_scr[...] = uc[HALO:HALO + tc, :GROUP]

    def tile(r0):
        acc = jnp.broadcast_to(ab_ref[...], (sub, GROUP))
        for r in range(SUBLANES):
            part = None
            for j in range(CONF_W):
                start = HALO - pad_a + j
                if start % SUBLANES != r:
                    continue
                lo = r0 + start - r
                term = aw_ref[j:j + 1, :] * ha_scr[lo:lo + sub + SUBLANES, :]
                part = term if part is None else part + term
            if part is not None:
                acc = acc + part[r:r + sub, :]
        mu = jnp.mean(acc, axis=-1, keepdims=True)
        xc = acc - mu
        y = xc * lax.rsqrt(jnp.mean(xc * xc, axis=-1, keepdims=True) + EPS) * lng_ref[...] + lnb_ref[...]
        y = y * _sigmoid(y)
        ya_ref[r0:r0 + sub, :] = _rms(y, gba_ref[...]).astype(BF16)

        acc = jnp.broadcast_to(cb_ref[...], (sub, GROUP))
        for j in range(SC_W):
            acc = acc + cw_ref[j:j + 1, :] * hc_scr[r0 + HALO - pad_c + j:r0 + HALO - pad_c + j + sub, :]
        y = gate_scr[r0:r0 + sub, :] * acc
        yc_ref[r0:r0 + sub, :] = _rms(y, gbc_ref[...]).astype(BF16)

    return fill, tile


def _hgrn_parts(gam_ref, qf_ref, zf_ref, vf_ref, qb_ref, zb_ref, vb_ref, of_ref, ob_ref,
                st_f, st_b, b_f, b_b, c_f, c_b, lam_f, lam_b, qe_f, qe_b, kt_f, kt_b,
                sts_f, sts_b, ones_scr, layer):
    tb = qf_ref.shape[1]
    nst = tb // STEP
    half = STEP // 2
    cs = 256

    ones_scr[...] = jnp.where(_head_ones(GROUP, HDIM), 1.0, 0.0).astype(BF16)

    ti = lax.broadcasted_iota(jnp.int32, (cs, cs), 0)
    si = lax.broadcasted_iota(jnp.int32, (cs, cs), 1)
    same = (ti // STEP) == (si // STEP)
    m_all = jnp.where(same, 1.0, 0.0).astype(BF16)

    def prep(d, q_ref, z_ref, b_scr, c_scr, lam_scr, qe_scr, kt_scr):
        g = gam_ref[d]
        e = jnp.exp(g - jnp.max(g, axis=0, keepdims=True))
        tot = jnp.sum(e, axis=0, keepdims=True)
        if layer == 0:
            lb = jnp.zeros_like(tot)
        else:
            lb = jnp.sum(e[1:layer + 1], axis=0, keepdims=True) / tot
        tri = (si <= ti) if d == 0 else (si >= ti)
        m_tri = jnp.where(same & tri, 1.0, 0.0).astype(BF16)
        for r0 in range(0, tb, cs):
            z = z_ref[0, r0:r0 + cs, :]
            a = jnp.exp(-jnp.abs(z))
            big = 1.0 / (1.0 + a)
            pos = z >= 0.0
            f = lb + (1.0 - lb) * jnp.where(pos, big, a * big)
            kk = (1.0 - lb) * jnp.where(pos, a * big, big)
            lf = jnp.log2(f)
            bloc = _split_dot(m_tri, lf)
            btot = _split_dot(m_all, lf)
            b_scr[r0:r0 + cs, :] = bloc
            c_scr[r0:r0 + cs, :] = bloc - jnp.log2(kk)
            lam_scr[r0:r0 + cs, :] = jnp.exp2(btot)
            qe_scr[r0:r0 + cs, :] = (q_ref[0, r0:r0 + cs, :] * jnp.exp2(bloc)).astype(BF16)
            kt_scr[r0:r0 + cs, :] = (kk * jnp.exp2(btot - bloc)).astype(BF16)

    prep(0, qf_ref, zf_ref, b_f, c_f, lam_f, qe_f, kt_f)
    prep(1, qb_ref, zb_ref, b_b, c_b, lam_b, qe_b, kt_b)

    trow = lax.broadcasted_iota(jnp.int32, (half, GROUP), 0)
    head_of_lane = lax.broadcasted_iota(jnp.int32, (STEP, GROUP), 1) // HDIM

    def liveness(s, rev):
        if rev:
            return (1, 0) if s < half else (2, 1)
        return (1, 2) if s < half else (0, 1)

    def per_head_rows(x):
        zero = jnp.zeros_like(x)
        return jnp.concatenate([jnp.where(head_of_lane == h, x, zero) for h in range(HEADS)], axis=0)

    def advance(k, r, state, v_ref, kt_scr, lam_scr, sts_scr):
        vb = v_ref[0, pl.ds(r, STEP), :].astype(BF16)
        vstack = jnp.concatenate([vb[:, h * HDIM:(h + 1) * HDIM] for h in range(HEADS)], axis=0)
        upd = _dot_tn(vstack, per_head_rows(kt_scr[pl.ds(r, STEP), :]))
        sts_scr[pl.ds(k * HDIM, HDIM), :] = state.astype(BF16)
        return lam_scr[pl.ds(r, 1), :] * state + upd

    def output(k, r, rev, q_ref, v_ref, o_ref, b_scr, c_scr, qe_scr, sts_scr):
        res = _dot_nt(per_head_rows(qe_scr[pl.ds(r, STEP), :]), sts_scr[pl.ds(k * HDIM, HDIM), :])
        o = jnp.concatenate([res[h * STEP:(h + 1) * STEP, :] for h in range(HEADS)], axis=1)
        o = [o[:half], o[half:]]
        q = (q_ref[0, pl.ds(r, half), :], q_ref[0, pl.ds(r + half, half), :])
        b = (b_scr[pl.ds(r, half), :], b_scr[pl.ds(r + half, half), :])
        pieces, where_to = [], []
        for s in range(STEP):
            crow = jnp.broadcast_to(c_scr[pl.ds(r + s, 1), :], (half, GROUP))
            for hx, kind in enumerate(liveness(s, rev)):
                if kind == 0:
                    continue
                dec = q[hx] * jnp.exp2(b[hx] - crow)
                if kind == 1:
                    sl = s - hx * half
                    dec = jnp.where((trow <= sl) if rev else (trow >= sl), dec, 0.0)
                pieces.append(dec)
                where_to.append((s, hx))
        att = _dot(jnp.concatenate(pieces, axis=0).astype(BF16), ones_scr[...])
        vrows = [jnp.broadcast_to(v_ref[0, pl.ds(r + s, 1), :], (half, GROUP)) for s in range(STEP)]
        for n, (s, hx) in enumerate(where_to):
            o[hx] = o[hx] + att[n * half:(n + 1) * half, :] * vrows[s]
        o_ref[0, pl.ds(r, half), :] = o[0]
        o_ref[0, pl.ds(r + half, half), :] = o[1]

    states = [st_f[...], st_b[...]]

    def state_step(k):
        states[0] = advance(k, k * STEP, states[0], vf_ref, kt_f, lam_f, sts_f)
        states[1] = advance(k, (nst - 1 - k) * STEP, states[1], vb_ref, kt_b, lam_b, sts_b)
        if k == nst - 1:
            st_f[...] = states[0]
            st_b[...] = states[1]

    def output_step(k):
        output(k, k * STEP, False, qf_ref, vf_ref, of_ref, b_f, c_f, qe_f, sts_f)
        output(k, (nst - 1 - k) * STEP, True, qb_ref, vb_ref, ob_ref, b_b, c_b, qe_b, sts_b)

    return nst, state_step, output_step


def _hgrn_scratch(tb):
    vm = lambda dt: pltpu.VMEM((tb, GROUP), dt)
    return ([pltpu.VMEM((HDIM, GROUP), F32)] * 2 + [vm(F32)] * 6 + [vm(BF16)] * 4
            + [pltpu.VMEM((tb // STEP * HDIM, GROUP), BF16)] * 2
            + [pltpu.VMEM((GROUP, GROUP), BF16)])


def _mla_qkv(ud, ct_ref, sg_ref, qag_ref, kvag_ref, wq_ref, wk_ref,
             gq_ref, gqs_ref, gk_ref, gks_ref, q_ref, k_ref, v_ref):
    nw = HEADS * HPAD
    nq = _rms(ud[:, :GROUP], qag_ref[...]).astype(BF16)
    qq = _dot(nq, wq_ref[...])
    nkv = _rms(ud[:, GROUP:GROUP + HPAD], kvag_ref[...]).astype(BF16)
    kr = ud[:, GROUP + HPAD:]
    kr_hi = kr.astype(BF16)
    kr_lo = (kr - kr_hi.astype(F32)).astype(BF16)
    kk = _dot(jnp.concatenate([nkv, kr_hi, kr_lo], axis=-1), wk_ref[...])
    ct = ct_ref[...]
    sg = sg_ref[...]
    scale = QK ** -0.5 * LOG2E
    for h in range(HEADS):
        sl = slice(h * HPAD, (h + 1) * HPAD)
        sw = slice(nw + h * HPAD, nw + (h + 1) * HPAD)
        x = qq[:, sl]
        r = lax.rsqrt(jnp.sum(x * x, axis=-1, keepdims=True) * (1.0 / QK) + EPS) * scale
        q_ref[:, sl] = (r * (x * (gq_ref[...] * ct) + qq[:, sw] * (gqs_ref[...] * sg))).astype(BF16)
        x = kk[:, sl]
        r = lax.rsqrt(jnp.sum(x * x, axis=-1, keepdims=True) * (1.0 / QK) + EPS)
        k_ref[:, sl] = (r * (x * (gk_ref[...] * ct) + kk[:, sw] * (gks_ref[...] * sg))).astype(BF16)
    first = lax.broadcasted_iota(jnp.int32, (ud.shape[0], HPAD), 1) < HDIM
    for pair in range(HEADS // 2):
        vv = kk[:, 2 * nw + pair * HPAD:2 * nw + (pair + 1) * HPAD]
        v_ref[2 * pair] = jnp.where(first, vv, 1.0).astype(BF16)
        v_ref[2 * pair + 1] = jnp.where(first, 1.0, vv).astype(BF16)


def _attn_parts(q_ref, k_ref, v_ref, gb_ref, o_ref, sb_scr, rows, ck):
    tq = q_ref.shape[1]
    n_keys = k_ref.shape[1]
    first = lax.broadcasted_iota(jnp.int32, (rows, HPAD), 1) < HDIM
    items = [(r0, h) for r0 in range(0, tq, rows) for h in range(HEADS)]
    done = []

    def scores(n):
        r0, h = items[n]
        sl = slice(h * HPAD, (h + 1) * HPAD)
        q = q_ref[0, r0:r0 + rows, sl]
        top = None
        for c in range(0, n_keys, ck):
            sb = _dot_nt(q, k_ref[0, c:c + ck, sl]).astype(BF16)
            sb_scr[n % 2, :, c:c + ck] = sb
            for j in range(0, ck, HPAD):
                top = sb[:, j:j + HPAD] if top is None else jnp.maximum(top, sb[:, j:j + HPAD])
        return jnp.max(top, axis=-1, keepdims=True)

    def finish(n, m):
        r0, h = items[n]
        w = None
        for c in range(0, n_keys, ck):
            p = jnp.exp2(sb_scr[n % 2, :, c:c + ck] - m)
            part = _dot(p, v_ref[h, 0, c:c + ck, :])
            w = part if w is None else w + part
        w = w / pltpu.roll(w, HDIM, axis=1)
        done.append(w)
        if h == HEADS - 1:
            pairs = [jnp.where(first, done[2 * j], done[2 * j + 1]) for j in range(HEADS // 2)]
            del done[:]
            o_ref[0, r0:r0 + rows, :] = _rms(jnp.concatenate(pairs, axis=1), gb_ref[...]).astype(BF16)

    return items, scores, finish


N_HGRN_SCRATCH = 15


def _mixers_kernel(*refs, layer, rows, key_chunk):
    (gam_ref, qf_ref, zf_ref, vf_ref, qb_ref, zb_ref, vb_ref, q_ref, k_ref, v_ref, gbd_ref,
     of_ref, ob_ref, yd_ref) = refs[:14]
    hgrn_scr = refs[14:14 + N_HGRN_SCRATCH]
    sb_scr, = refs[14 + N_HGRN_SCRATCH:]

    @pl.when(pl.program_id(1) == 0)
    def _():
        for st in hgrn_scr[:2]:
            st[...] = jnp.zeros_like(st)

    nst, state_step, output_step = _hgrn_parts(gam_ref, qf_ref, zf_ref, vf_ref, qb_ref, zb_ref, vb_ref,
                                               of_ref, ob_ref, *hgrn_scr, layer)
    items, scores, finish = _attn_parts(q_ref, k_ref, v_ref, gbd_ref, yd_ref, sb_scr, rows, key_chunk)

    def share(seq, n, first, count):
        n -= first
        return seq[n * len(seq) // count:(n + 1) * len(seq) // count] if 0 <= n < count else []

    state_items = len(items) // 2
    first_out = len(items) // 4
    m_next = scores(0)
    for n in range(len(items)):
        m = m_next
        if n + 1 < len(items):
            m_next = scores(n + 1)
        finish(n, m)
        for k in share(range(nst), n, 0, state_items):
            state_step(k)
        for k in share(range(nst), n, first_out, len(items) - first_out):
            output_step(k)


def _mixers(ub, q, k, v, gamma, gbd, layer, tile):
    b, s, _ = ub.shape
    nb = s // tile
    col = lambda j, rev: pl.BlockSpec(
        (1, tile, GROUP), (lambda bi, i: (bi, nb - 1 - i, j)) if rev else (lambda bi, i: (bi, i, j)))
    main = lambda w: pl.BlockSpec((1, tile, w), lambda bi, i: (bi, i, 0))
    nw = q.shape[2]
    return pl.pallas_call(
        functools.partial(_mixers_kernel, layer=layer, rows=SUB_ROWS, key_chunk=KEY_CHUNK),
        grid=(b, nb),
        in_specs=[_full(gamma.shape),
                  col(0, False), col(1, False), col(3, False), col(0, True), col(2, True), col(3, True),
                  main(nw), pl.BlockSpec((1, s, nw), lambda bi, i: (bi, 0, 0), pipeline_mode=pl.Buffered(1)),
                  pl.BlockSpec((HEADS, 1, s, HPAD), lambda bi, i: (0, bi, 0, 0), pipeline_mode=pl.Buffered(1)),
                  _full(gbd.shape)],
        out_specs=[main(GROUP), pl.BlockSpec((1, tile, GROUP), lambda bi, i: (bi, nb - 1 - i, 0)), main(GROUP)],
        out_shape=[jax.ShapeDtypeStruct((b, s, GROUP), F32)] * 2 + [jax.ShapeDtypeStruct((b, s, GROUP), BF16)],
        scratch_shapes=_hgrn_scratch(tile) + [pltpu.VMEM((2, SUB_ROWS, s), BF16)],
        compiler_params=_params("parallel", "arbitrary"),
        name="mixers",
    )(gamma, ub, ub, ub, ub, ub, ub, q, k, v, gbd)


def _channel_kernel(x_ref, ya_ref, of_ref, ob_ref, g_ref, yc_ref, yd_ref, on_ref, gbb_ref, wo_ref,
                    gxq_ref, wq_ref, qn_ref, kx_ref, vx_ref, xwo_ref, gf_ref, w13_ref, w2_ref, o_ref, *, rows):
    tm = x_ref.shape[1]
    dff = w2_ref.shape[1]
    ones_bd = jnp.where(_head_ones(GROUP, HDIM), 1.0, 0.0).astype(BF16)
    lane = lax.broadcasted_iota(jnp.int32, (rows, GROUP), 1) // HDIM

    def mix(r0):
        sl = slice(r0, r0 + rows)
        o = of_ref[0, sl, :] + ob_ref[0, sl, :]
        g = g_ref[0, sl, :]
        yb = o * _head_rsqrt(o, ones_bd, HDIM) * on_ref[...] * (g * _sigmoid(g))
        yb = _rms(yb, gbb_ref[...]).astype(BF16)
        x1 = x_ref[0, sl, :]
        for j, y in enumerate((ya_ref[0, sl, :], yb, yc_ref[0, sl, :], yd_ref[0, sl, :])):
            x1 = x1 + _dot(y, wo_ref[0, j * GROUP:(j + 1) * GROUP, :])
        q = _dot(_rms(x1, gxq_ref[...]).astype(BF16), wq_ref[0])
        qn = q * _head_rsqrt(q, ones_bd, HDIM) * (qn_ref[...] * HDIM ** -0.5)
        acc = jnp.zeros((rows, GROUP), F32)
        for h in range(HEADS):
            s = _dot_nt(jnp.where(lane == h, qn, 0.0).astype(BF16), kx_ref[0, 0])
            p = jnp.exp(s - jnp.max(s, axis=-1, keepdims=True))
            l = jnp.sum(p, axis=-1, keepdims=True)
            pv = _dot(p.astype(BF16), vx_ref[0, 0])
            acc = jnp.where(lane == h, pv / l, acc)
        return x1 + _dot(acc.astype(BF16), xwo_ref[0])

    def ffn(x):
        a = _dot(_rms(x, gf_ref[...]).astype(BF16), w13_ref[0])
        a1 = a[:, :dff]
        return x + _dot((a1 * _sigmoid(a1) * a[:, dff:]).astype(BF16), w2_ref[0])

    starts = range(0, tm, rows)
    mixed = [mix(r0) for r0 in starts]
    for r0, x in zip(starts, mixed):
        o_ref[0, r0:r0 + rows, :] = ffn(x)


def _channel(x, ya, of, ob, ub, yc, yd, on, gbb, wo, gxq, wq, qn, kx, vx, xwo, gf, w13, w2, layer, tm, rows):
    b, s, d = x.shape
    n = kx.shape[2]
    row = lambda w, j=0: pl.BlockSpec((1, tm, w), lambda bi, i: (bi, i, j))
    of_layer = lambda a: pl.BlockSpec((1,) + a.shape[1:], lambda bi, i: (layer,) + (0,) * (a.ndim - 1),
                                      pipeline_mode=pl.Buffered(1))
    return pl.pallas_call(
        functools.partial(_channel_kernel, rows=rows),
        grid=(b, s // tm),
        in_specs=[row(d), row(GROUP), row(GROUP), row(GROUP), row(GROUP, 4), row(GROUP), row(GROUP),
                  _full(on.shape), _full(gbb.shape), of_layer(wo), _full(gxq.shape), of_layer(wq),
                  _full(qn.shape),
                  pl.BlockSpec((1, 1, n, GROUP), lambda bi, i: (layer, bi, 0, 0)),
                  pl.BlockSpec((1, 1, n, GROUP), lambda bi, i: (layer, bi, 0, 0)),
                  of_layer(xwo), _full(gf.shape), of_layer(w13), of_layer(w2)],
        out_specs=row(d),
        out_shape=jax.ShapeDtypeStruct((b, s, d), F32),
        compiler_params=_params("parallel", "parallel"),
        name="channel",
    )(x, ya, of, ob, ub, yc, yd, on, gbb, wo, gxq, wq, qn, kx, vx, xwo, gf, w13, w2)


def _pad_heads(w, width):
    lead = w.shape[:-1]
    w = w.reshape(lead + (HEADS, width))
    w = jnp.pad(w, [(0, 0)] * len(lead) + [(0, 0), (0, HPAD - width)])
    return w.reshape(lead + (HEADS * HPAD,))


def _swap_rope(w):
    half = ROPE // 2
    z = jnp.zeros_like(w[..., :HDIM])
    return jnp.concatenate([z, w[..., HDIM + half:], w[..., HDIM:HDIM + half]], axis=-1)


def _mla_weights(wuq, wukv, qn_g, kn_g):
    dq = wuq.shape[0]
    dkv = wukv.shape[0]
    wq3 = wuq.reshape(dq, HEADS, QK)
    wq = jnp.concatenate([_pad_heads(wuq, QK), _pad_heads(_swap_rope(wq3).reshape(dq, -1), QK)], axis=1)
    kv3 = wukv.reshape(dkv, HEADS, 2 * HDIM)
    k_nope = jnp.pad(kv3[..., :HDIM], [(0, 0), (0, 0), (0, HPAD - HDIM)]).reshape(dkv, -1)
    v_cols = kv3[..., HDIM:].reshape(dkv, -1)
    top = jnp.concatenate([k_nope, jnp.zeros_like(k_nope), v_cols], axis=1)
    eye = jnp.eye(ROPE, dtype=F32)
    place = jnp.pad(eye, [(0, 0), (HDIM, HPAD - QK)])
    place_sw = jnp.pad(jnp.roll(eye, ROPE // 2, axis=1), [(0, 0), (HDIM, HPAD - QK)])
    rope_rows = jnp.concatenate([jnp.tile(place, (1, HEADS)), jnp.tile(place_sw, (1, HEADS)),
                                 jnp.zeros((ROPE, v_cols.shape[1]), F32)], axis=1)
    rope_rows = jnp.pad(rope_rows, [(0, HPAD - ROPE), (0, 0)])
    wk = jnp.concatenate([top, rope_rows, rope_rows], axis=0)
    pad1 = lambda g: jnp.pad(g, (0, HPAD - QK))[None, :]
    return (wq.astype(BF16), wk.astype(BF16),
            pad1(qn_g), pad1(_swap_rope(qn_g)), pad1(kn_g), pad1(_swap_rope(kn_g)))


def kernel(x, mem, positions, g_mix, w_in, a_dw_w, a_dw_b, a_ln_g, a_ln_b, h_gamma, h_onorm_g, c_dw_w, c_dw_b, m_qa_g, m_wuq, m_kva_g, m_wukv, m_qn_g, m_kn_g, g_branch, w_out, g_xq, g_mem, x_wq, x_wkv, x_qn_g, x_kn_g, x_wo, g_ffn, f_w13, f_w2):
    b, s, d = x.shape
    t = b * s
    depth = w_in.shape[0]
    row = lambda v: v[None, :]
    tile4 = lambda v: jnp.tile(v, HEADS)[None, :]

    inv = ROPE_BASE ** (-jnp.arange(0, ROPE, 2, dtype=F32) / ROPE)
    inv_row = jnp.concatenate([jnp.zeros((HDIM,), F32), inv, inv, jnp.zeros((HPAD - QK,), F32)])[None, :]
    ct, sg = _rope_tables(positions.astype(F32).reshape(t, 1), inv_row, TOKEN_TILE)

    kv4 = x_wkv.reshape(depth, d, HEADS, 2, HDIM).transpose(0, 1, 3, 2, 4).reshape(depth, d, 2 * GROUP)
    kx, vx = _mem_kv(mem, g_mem[:, None, :], kv4.astype(BF16), jnp.tile(x_kn_g, (1, HEADS))[:, None, :])

    wo_all, xwq_all, xwo_all = w_out.astype(BF16), x_wq.astype(BF16), x_wo.astype(BF16)
    w13_all, w2_all = f_w13.astype(BF16), f_w2.astype(BF16)
    widths = (2 * GROUP, 5 * GROUP, 3 * GROUP, 2 * GROUP)
    w_in_all = jnp.pad(w_in, [(0, 0), (0, 0), (0, sum(widths) - w_in.shape[2])]).astype(BF16)
    for l in range(depth):
        mla = (row(m_qa_g[l]), row(m_kva_g[l])) + _mla_weights(m_wuq[l], m_wukv[l], m_qn_g[l], m_kn_g[l])
        gb = g_branch[l]
        conv = (a_dw_w[l], row(a_dw_b[l]), row(a_ln_g[l]), row(a_ln_b[l]), c_dw_w[l], row(c_dw_b[l]),
                row(gb[:GROUP]), row(gb[2 * GROUP:3 * GROUP]))
        ub, qr, kr, vr, ya, yc = _in_proj(x.reshape(t, d), row(g_mix[l]), w_in_all, ct, sg, mla, conv,
                                          widths, l, TOKEN_TILE, s)
        ub = ub.reshape(b, s, -1)
        of, ob, yd = _mixers(ub, qr.reshape(b, s, -1), kr.reshape(b, s, -1), vr.reshape(HEADS, b, s, HPAD),
                             h_gamma, row(gb[3 * GROUP:]), l, TOKEN_TILE)
        ya = ya.reshape(b, s, -1)
        yc = yc.reshape(b, s, -1)
        x = _channel(x, ya, of, ob, ub, yc, yd, row(h_onorm_g[l]), row(gb[GROUP:2 * GROUP]), wo_all,
                     row(g_xq[l]), xwq_all, tile4(x_qn_g[l]), kx, vx, xwo_all, row(g_ffn[l]), w13_all, w2_all,
                     l, TOKEN_TILE, SUB_ROWS)
    return x
```

```python
import functools

import jax
import jax.numpy as jnp
from jax import lax
from jax.experimental import pallas as pl
from jax.experimental.pallas import tpu as pltpu

F32 = jnp.float32
BF16 = jnp.bfloat16
EPS = 1e-6

GROUP = 256
HEADS = 4
HDIM = 64
ROPE = 32
QK = HDIM + ROPE
HPAD = 128
CONF_W = 31
SC_W = 3
HALO = 16
STEP = 16
SUBLANES = 8
ROPE_BASE = 10000.0
LOG2E = 1.4426950408889634
VMEM_LIMIT = 60 * 1024 * 1024

TOKEN_TILE = 512
SUB_ROWS = 256
KEY_CHUNK = 512
CONV_ROWS = 64


def _rms(x, g):
    return x * lax.rsqrt(jnp.mean(x * x, axis=-1, keepdims=True) + EPS) * g


def _sigmoid(x):
    return 1.0 / (1.0 + jnp.exp(-x))


def _dot(a, b):
    return jnp.dot(a, b, preferred_element_type=F32)


def _dot_nt(a, b):
    return lax.dot_general(a, b, (((1,), (1,)), ((), ())), preferred_element_type=F32)


def _dot_tn(a, b):
    return lax.dot_general(a, b, (((0,), (0,)), ((), ())), preferred_element_type=F32)


def _split_dot(m, x):
    hi = x.astype(BF16)
    lo = (x - hi.astype(F32)).astype(BF16)
    return _dot(m, hi) + _dot(m, lo)


def _head_ones(n, width):
    r = lax.broadcasted_iota(jnp.int32, (n, n), 0) // width
    c = lax.broadcasted_iota(jnp.int32, (n, n), 1) // width
    return r == c


def _head_rsqrt(x, ones_bd, width):
    return lax.rsqrt(_split_dot_rhs(x * x, ones_bd) * (1.0 / width) + EPS)


def _split_dot_rhs(x, m):
    hi = x.astype(BF16)
    lo = (x - hi.astype(F32)).astype(BF16)
    return _dot(hi, m) + _dot(lo, m)


def _params(*sem):
    return pltpu.CompilerParams(dimension_semantics=sem, vmem_limit_bytes=VMEM_LIMIT)


def _full(shape):
    nd = len(shape)
    return pl.BlockSpec(shape, lambda *_: (0,) * nd)


def _rope_kernel(pos_ref, inv_ref, ct_ref, sg_ref):
    ang = pos_ref[...] * inv_ref[...]
    lane = lax.broadcasted_iota(jnp.int32, ang.shape, 1)
    c = jnp.cos(ang)
    s = jnp.sin(ang)
    half = ROPE // 2
    ct_ref[...] = jnp.where(lane < HDIM, 1.0, jnp.where(lane < QK, c, 0.0))
    sg_ref[...] = jnp.where(lane < HDIM, 0.0,
                            jnp.where(lane < HDIM + half, -s, jnp.where(lane < QK, s, 0.0)))


def _rope_tables(pos_f, inv_row, tm):
    t = pos_f.shape[0]
    return pl.pallas_call(
        _rope_kernel,
        grid=(t // tm,),
        in_specs=[pl.BlockSpec((tm, 1), lambda i: (i, 0)), _full((1, HPAD))],
        out_specs=[pl.BlockSpec((tm, HPAD), lambda i: (i, 0))] * 2,
        out_shape=[jax.ShapeDtypeStruct((t, HPAD), F32)] * 2,
        compiler_params=_params("parallel"),
        name="rope_tables",
    )(pos_f, inv_row)


def _memkv_kernel(mem_ref, g_ref, w_ref, kn_ref, k_ref, v_ref):
    mn = _rms(mem_ref[0], g_ref[0]).astype(BF16)
    kv = _dot(mn, w_ref[0])
    k = kv[:, :GROUP]
    ones_bd = jnp.where(_head_ones(GROUP, HDIM), 1.0, 0.0).astype(BF16)
    k_ref[0, 0] = (k * _head_rsqrt(k, ones_bd, HDIM) * kn_ref[0]).astype(BF16)
    v_ref[0, 0] = kv[:, GROUP:].astype(BF16)


def _mem_kv(mem, g_mem, wkv, kn_g):
    b, n, d = mem.shape
    nl = wkv.shape[0]
    return pl.pallas_call(
        _memkv_kernel,
        grid=(nl, b),
        in_specs=[pl.BlockSpec((1, n, d), lambda l, i: (i, 0, 0)),
                  pl.BlockSpec((1, 1, d), lambda l, i: (l, 0, 0)),
                  pl.BlockSpec((1, d, 2 * GROUP), lambda l, i: (l, 0, 0)),
                  pl.BlockSpec((1, 1, GROUP), lambda l, i: (l, 0, 0))],
        out_specs=[pl.BlockSpec((1, 1, n, GROUP), lambda l, i: (l, i, 0, 0))] * 2,
        out_shape=[jax.ShapeDtypeStruct((nl, b, n, GROUP), BF16)] * 2,
        compiler_params=_params("parallel", "parallel"),
        name="mem_kv",
    )(mem, g_mem, wkv, kn_g)


def _inproj_kernel(x_ref, xp_ref, xn_ref, g_ref, w_ref, ct_ref, sg_ref, qag_ref, kvag_ref, wq_ref, wk_ref,
                   gq_ref, gqs_ref, gk_ref, gks_ref, aw_ref, ab_ref, lng_ref, lnb_ref, cw_ref, cb_ref, gba_ref, gbc_ref,
                   ub_ref, q_ref, k_ref, v_ref, ya_ref, yc_ref, ha_scr, hc_scr, gate_scr,
                   *, widths, tiles_per_seq, conv_rows):
    wa, wb, wc, wd = widths
    i = pl.program_id(0) % tiles_per_seq
    norm = lambda x: _rms(x, g_ref[...]).astype(BF16)
    n = norm(x_ref[...])
    _mla_qkv(_dot(n, w_ref[0, :, wa + wb + wc:wa + wb + wc + wd]), ct_ref, sg_ref, qag_ref, kvag_ref, wq_ref, wk_ref,
             gq_ref, gqs_ref, gk_ref, gks_ref, q_ref, k_ref, v_ref)
    n_ext = jnp.concatenate([norm(xp_ref[...]), n, norm(xn_ref[...])], axis=0)
    conv_fill, conv_tile = _conv_parts(aw_ref, ab_ref, lng_ref, lnb_ref, cw_ref, cb_ref, gba_ref, gbc_ref,
                                       ya_ref, yc_ref, ha_scr, hc_scr, gate_scr, conv_rows)
    conv_fill(_dot(n_ext, w_ref[0, :, 0:wa]), _dot(n_ext, w_ref[0, :, wa + wb:wa + wb + wc]),
              jnp.where(i > 0, 1.0, 0.0), jnp.where(i < tiles_per_seq - 1, 1.0, 0.0))
    ub_ref[...] = _dot(n, w_ref[0, :, wa:wa + wb])
    for r0 in range(0, x_ref.shape[0], conv_rows):
        conv_tile(r0)


def _in_proj(x2, g, w, ct, sg, mla, conv, widths, layer, tm, seq):
    t, d = x2.shape
    nw = HEADS * HPAD
    hb = tm // HALO
    last = t // HALO - 1
    row = lambda n: pl.BlockSpec((tm, n), lambda i: (i, 0))
    return pl.pallas_call(
        functools.partial(_inproj_kernel, widths=widths, tiles_per_seq=seq // tm, conv_rows=CONV_ROWS),
        grid=(t // tm,),
        in_specs=[row(d),
                  pl.BlockSpec((HALO, d), lambda i: (jnp.maximum(i * hb - 1, 0), 0)),
                  pl.BlockSpec((HALO, d), lambda i: (jnp.minimum((i + 1) * hb, last), 0)),
                  _full((1, d)),
                  pl.BlockSpec((1,) + w.shape[1:], lambda i: (layer, 0, 0), pipeline_mode=pl.Buffered(1)),
                  row(HPAD), row(HPAD)] + [_full(a.shape) for a in mla + conv],
        out_specs=[row(widths[1]), row(nw), row(nw), pl.BlockSpec((HEADS, tm, HPAD), lambda i: (0, i, 0)),
                   row(GROUP), row(GROUP)],
        out_shape=[jax.ShapeDtypeStruct((t, widths[1]), F32),
                   jax.ShapeDtypeStruct((t, nw), BF16), jax.ShapeDtypeStruct((t, nw), BF16),
                   jax.ShapeDtypeStruct((HEADS, t, HPAD), BF16),
                   jax.ShapeDtypeStruct((t, GROUP), BF16), jax.ShapeDtypeStruct((t, GROUP), BF16)],
        scratch_shapes=[pltpu.VMEM((tm + 2 * HALO, GROUP), F32)] * 2 + [pltpu.VMEM((tm, GROUP), F32)],
        compiler_params=_params("parallel"),
        name="in_proj",
    )(x2, x2, x2, g, w, ct, sg, *mla, *conv)


def _conv_parts(aw_ref, ab_ref, lng_ref, lnb_ref, cw_ref, cb_ref, gba_ref, gbc_ref,
                ya_ref, yc_ref, ha_scr, hc_scr, gate_scr, sub):
    tc = gate_scr.shape[0]
    pad_a = (CONF_W - 1) // 2
    pad_c = (SC_W - 1) // 2

    def fill(ua, uc, has_prev, has_next):
        ha = ua[:, :GROUP] * _sigmoid(ua[:, GROUP:])
        hc = uc[:, GROUP:2 * GROUP] * uc[:, 2 * GROUP:]
        for scr, h in ((ha_scr, ha), (hc_scr, hc)):
            scr[0:HALO, :] = h[0:HALO] * has_prev
            scr[HALO:HALO + tc, :] = h[HALO:HALO + tc]
            scr[HALO + tc:, :] = h[HALO + tc:] * has_next
        gate_scr[...] = uc[HALO:HALO + tc, :GROUP]

    def tile(r0):
        acc = jnp.broadcast_to(ab_ref[...], (sub, GROUP))
        for r in range(SUBLANES):
            part = None
            for j in range(CONF_W):
                start = HALO - pad_a + j
                if start % SUBLANES != r:
                    continue
                lo = r0 + start - r
                term = aw_ref[j:j + 1, :] * ha_scr[lo:lo + sub + SUBLANES, :]
                part = term if part is None else part + term
            if part is not None:
                acc = acc + part[r:r + sub, :]
        mu = jnp.mean(acc, axis=-1, keepdims=True)
        xc = acc - mu
        y = xc * lax.rsqrt(jnp.mean(xc * xc, axis=-1, keepdims=True) + EPS) * lng_ref[...] + lnb_ref[...]
        y = y * _sigmoid(y)
        ya_ref[r0:r0 + sub, :] = _rms(y, gba_ref[...]).astype(BF16)

        acc = jnp.broadcast_to(cb_ref[...], (sub, GROUP))
        for j in range(SC_W):
            acc = acc + cw_ref[j:j + 1, :] * hc_scr[r0 + HALO - pad_c + j:r0 + HALO - pad_c + j + sub, :]
        y = gate_scr[r0:r0 + sub, :] * acc
        yc_ref[r0:r0 + sub, :] = _rms(y, gbc_ref[...]).astype(BF16)

    return fill, tile


def _hgrn_parts(gam_ref, qf_ref, zf_ref, vf_ref, qb_ref, zb_ref, vb_ref, of_ref, ob_ref,
                st_f, st_b, b_f, b_b, c_f, c_b, lam_f, lam_b, qe_f, qe_b, kt_f, kt_b,
                sts_f, sts_b, ones_scr, layer):
    tb = qf_ref.shape[1]
    nst = tb // STEP
    half = STEP // 2
    cs = 256

    ones_scr[...] = jnp.where(_head_ones(GROUP, HDIM), 1.0, 0.0).astype(BF16)

    ti = lax.broadcasted_iota(jnp.int32, (cs, cs), 0)
    si = lax.broadcasted_iota(jnp.int32, (cs, cs), 1)
    same = (ti // STEP) == (si // STEP)
    m_all = jnp.where(same, 1.0, 0.0).astype(BF16)

    def prep(d, q_ref, z_ref, b_scr, c_scr, lam_scr, qe_scr, kt_scr):
        g = gam_ref[d]
        e = jnp.exp(g - jnp.max(g, axis=0, keepdims=True))
        tot = jnp.sum(e, axis=0, keepdims=True)
        if layer == 0:
            lb = jnp.zeros_like(tot)
        else:
            lb = jnp.sum(e[1:layer + 1], axis=0, keepdims=True) / tot
        tri = (si <= ti) if d == 0 else (si >= ti)
        m_tri = jnp.where(same & tri, 1.0, 0.0).astype(BF16)
        for r0 in range(0, tb, cs):
            z = z_ref[0, r0:r0 + cs, :]
            a = jnp.exp(-jnp.abs(z))
            big = 1.0 / (1.0 + a)
            pos = z >= 0.0
            f = lb + (1.0 - lb) * jnp.where(pos, big, a * big)
            kk = (1.0 - lb) * jnp.where(pos, a * big, big)
            lf = jnp.log2(f)
            bloc = _split_dot(m_tri, lf)
            btot = _split_dot(m_all, lf)
            b_scr[r0:r0 + cs, :] = bloc
            c_scr[r0:r0 + cs, :] = bloc - jnp.log2(kk)
            lam_scr[r0:r0 + cs, :] = jnp.exp2(btot)
            qe_scr[r0:r0 + cs, :] = (q_ref[0, r0:r0 + cs, :] * jnp.exp2(bloc)).astype(BF16)
            kt_scr[r0:r0 + cs, :] = (kk * jnp.exp2(btot - bloc)).astype(BF16)

    prep(0, qf_ref, zf_ref, b_f, c_f, lam_f, qe_f, kt_f)
    prep(1, qb_ref, zb_ref, b_b, c_b, lam_b, qe_b, kt_b)

    trow = lax.broadcasted_iota(jnp.int32, (half, GROUP), 0)
    head_of_lane = lax.broadcasted_iota(jnp.int32, (STEP, GROUP), 1) // HDIM

    def liveness(s, rev):
        if rev:
            return (1, 0) if s < half else (2, 1)
        return (1, 2) if s < half else (0, 1)

    def per_head_rows(x):
        zero = jnp.zeros_like(x)
        return jnp.concatenate([jnp.where(head_of_lane == h, x, zero) for h in range(HEADS)], axis=0)

    def advance(k, r, state, v_ref, kt_scr, lam_scr, sts_scr):
        vb = v_ref[0, pl.ds(r, STEP), :].astype(BF16)
        vstack = jnp.concatenate([vb[:, h * HDIM:(h + 1) * HDIM] for h in range(HEADS)], axis=0)
        upd = _dot_tn(vstack, per_head_rows(kt_scr[pl.ds(r, STEP), :]))
        sts_scr[pl.ds(k * HDIM, HDIM), :] = state.astype(BF16)
        return lam_scr[pl.ds(r, 1), :] * state + upd

    def output(k, r, rev, q_ref, v_ref, o_ref, b_scr, c_scr, qe_scr, sts_scr):
        res = _dot_nt(per_head_rows(qe_scr[pl.ds(r, STEP), :]), sts_scr[pl.ds(k * HDIM, HDIM), :])
        o = jnp.concatenate([res[h * STEP:(h + 1) * STEP, :] for h in range(HEADS)], axis=1)
        o = [o[:half], o[half:]]
        q = (q_ref[0, pl.ds(r, half), :], q_ref[0, pl.ds(r + half, half), :])
        b = (b_scr[pl.ds(r, half), :], b_scr[pl.ds(r + half, half), :])
        pieces, where_to = [], []
        for s in range(STEP):
            crow = jnp.broadcast_to(c_scr[pl.ds(r + s, 1), :], (half, GROUP))
            for hx, kind in enumerate(liveness(s, rev)):
                if kind == 0:
                    continue
                dec = q[hx] * jnp.exp2(b[hx] - crow)
                if kind == 1:
                    sl = s - hx * half
                    dec = jnp.where((trow <= sl) if rev else (trow >= sl), dec, 0.0)
                pieces.append(dec)
                where_to.append((s, hx))
        att = _dot(jnp.concatenate(pieces, axis=0).astype(BF16), ones_scr[...])
        vrows = [jnp.broadcast_to(v_ref[0, pl.ds(r + s, 1), :], (half, GROUP)) for s in range(STEP)]
        for n, (s, hx) in enumerate(where_to):
            o[hx] = o[hx] + att[n * half:(n + 1) * half, :] * vrows[s]
        o_ref[0, pl.ds(r, half), :] = o[0]
        o_ref[0, pl.ds(r + half, half), :] = o[1]

    states = [st_f[...], st_b[...]]

    def state_step(k):
        states[0] = advance(k, k * STEP, states[0], vf_ref, kt_f, lam_f, sts_f)
        states[1] = advance(k, (nst - 1 - k) * STEP, states[1], vb_ref, kt_b, lam_b, sts_b)
        if k == nst - 1:
            st_f[...] = states[0]
            st_b[...] = states[1]

    def output_step(k):
        output(k, k * STEP, False, qf_ref, vf_ref, of_ref, b_f, c_f, qe_f, sts_f)
        output(k, (nst - 1 - k) * STEP, True, qb_ref, vb_ref, ob_ref, b_b, c_b, qe_b, sts_b)

    return nst, state_step, output_step


def _hgrn_scratch(tb):
    vm = lambda dt: pltpu.VMEM((tb, GROUP), dt)
    return ([pltpu.VMEM((HDIM, GROUP), F32)] * 2 + [vm(F32)] * 6 + [vm(BF16)] * 4
            + [pltpu.VMEM((tb // STEP * HDIM, GROUP), BF16)] * 2
            + [pltpu.VMEM((GROUP, GROUP), BF16)])


def _mla_qkv(ud, ct_ref, sg_ref, qag_ref, kvag_ref, wq_ref, wk_ref,
             gq_ref, gqs_ref, gk_ref, gks_ref, q_ref, k_ref, v_ref):
    nw = HEADS * HPAD
    nq = _rms(ud[:, :GROUP], qag_ref[...]).astype(BF16)
    qq = _dot(nq, wq_ref[...])
    nkv = _rms(ud[:, GROUP:GROUP + HPAD], kvag_ref[...]).astype(BF16)
    kr = ud[:, GROUP + HPAD:]
    kr_hi = kr.astype(BF16)
    kr_lo = (kr - kr_hi.astype(F32)).astype(BF16)
    kk = _dot(jnp.concatenate([nkv, kr_hi, kr_lo], axis=-1), wk_ref[...])
    ct = ct_ref[...]
    sg = sg_ref[...]
    scale = QK ** -0.5 * LOG2E
    for h in range(HEADS):
        sl = slice(h * HPAD, (h + 1) * HPAD)
        sw = slice(nw + h * HPAD, nw + (h + 1) * HPAD)
        x = qq[:, sl]
        r = lax.rsqrt(jnp.sum(x * x, axis=-1, keepdims=True) * (1.0 / QK) + EPS) * scale
        q_ref[:, sl] = (r * (x * (gq_ref[...] * ct) + qq[:, sw] * (gqs_ref[...] * sg))).astype(BF16)
        x = kk[:, sl]
        r = lax.rsqrt(jnp.sum(x * x, axis=-1, keepdims=True) * (1.0 / QK) + EPS)
        k_ref[:, sl] = (r * (x * (gk_ref[...] * ct) + kk[:, sw] * (gks_ref[...] * sg))).astype(BF16)
    first = lax.broadcasted_iota(jnp.int32, (ud.shape[0], HPAD), 1) < HDIM
    for pair in range(HEADS // 2):
        vv = kk[:, 2 * nw + pair * HPAD:2 * nw + (pair + 1) * HPAD]
        v_ref[2 * pair] = jnp.where(first, vv, 1.0).astype(BF16)
        v_ref[2 * pair + 1] = jnp.where(first, 1.0, vv).astype(BF16)


def _attn_parts(q_ref, k_ref, v_ref, gb_ref, o_ref, sb_scr, rows, ck):
    tq = q_ref.shape[1]
    n_keys = k_ref.shape[1]
    first = lax.broadcasted_iota(jnp.int32, (rows, HPAD), 1) < HDIM
    items = [(r0, h) for r0 in range(0, tq, rows) for h in range(HEADS)]
    done = []

    def scores(n):
        r0, h = items[n]
        sl = slice(h * HPAD, (h + 1) * HPAD)
        q = q_ref[0, r0:r0 + rows, sl]
        top = None
        for c in range(0, n_keys, ck):
            sb = _dot_nt(q, k_ref[0, c:c + ck, sl]).astype(BF16)
            sb_scr[n % 2, :, c:c + ck] = sb
            for j in range(0, ck, HPAD):
                top = sb[:, j:j + HPAD] if top is None else jnp.maximum(top, sb[:, j:j + HPAD])
        return jnp.max(top, axis=-1, keepdims=True)

    def finish(n, m):
        r0, h = items[n]
        w = None
        for c in range(0, n_keys, ck):
            p = jnp.exp2(sb_scr[n % 2, :, c:c + ck] - m)
            part = _dot(p, v_ref[h, 0, c:c + ck, :])
            w = part if w is None else w + part
        w = w / pltpu.roll(w, HDIM, axis=1)
        done.append(w)
        if h == HEADS - 1:
            pairs = [jnp.where(first, done[2 * j], done[2 * j + 1]) for j in range(HEADS // 2)]
            del done[:]
            o_ref[0, r0:r0 + rows, :] = _rms(jnp.concatenate(pairs, axis=1), gb_ref[...]).astype(BF16)

    return items, scores, finish


N_HGRN_SCRATCH = 15


def _mixers_kernel(*refs, layer, rows, key_chunk):
    (gam_ref, qf_ref, zf_ref, vf_ref, qb_ref, zb_ref, vb_ref, q_ref, k_ref, v_ref, gbd_ref,
     of_ref, ob_ref, yd_ref) = refs[:14]
    hgrn_scr = refs[14:14 + N_HGRN_SCRATCH]
    sb_scr, = refs[14 + N_HGRN_SCRATCH:]

    @pl.when(pl.program_id(1) == 0)
    def _():
        for st in hgrn_scr[:2]:
            st[...] = jnp.zeros_like(st)

    nst, state_step, output_step = _hgrn_parts(gam_ref, qf_ref, zf_ref, vf_ref, qb_ref, zb_ref, vb_ref,
                                               of_ref, ob_ref, *hgrn_scr, layer)
    items, scores, finish = _attn_parts(q_ref, k_ref, v_ref, gbd_ref, yd_ref, sb_scr, rows, key_chunk)

    def share(seq, n, first, count):
        n -= first
        return seq[n * len(seq) // count:(n + 1) * len(seq) // count] if 0 <= n < count else []

    state_items = len(items) // 2
    first_out = len(items) // 4
    m_next = scores(0)
    for n in range(len(items)):
        m = m_next
        if n + 1 < len(items):
            m_next = scores(n + 1)
        finish(n, m)
        for k in share(range(nst), n, 0, state_items):
            state_step(k)
        for k in share(range(nst), n, first_out, len(items) - first_out):
            output_step(k)


def _mixers(ub, q, k, v, gamma, gbd, layer, tile):
    b, s, _ = ub.shape
    nb = s // tile
    col = lambda j, rev: pl.BlockSpec(
        (1, tile, GROUP), (lambda bi, i: (bi, nb - 1 - i, j)) if rev else (lambda bi, i: (bi, i, j)))
    main = lambda w: pl.BlockSpec((1, tile, w), lambda bi, i: (bi, i, 0))
    nw = q.shape[2]
    return pl.pallas_call(
        functools.partial(_mixers_kernel, layer=layer, rows=SUB_ROWS, key_chunk=KEY_CHUNK),
        grid=(b, nb),
        in_specs=[_full(gamma.shape),
                  col(0, False), col(1, False), col(3, False), col(0, True), col(2, True), col(3, True),
                  main(nw), pl.BlockSpec((1, s, nw), lambda bi, i: (bi, 0, 0)),
                  pl.BlockSpec((HEADS, 1, s, HPAD), lambda bi, i: (0, bi, 0, 0)),
                  _full(gbd.shape)],
        out_specs=[main(GROUP), pl.BlockSpec((1, tile, GROUP), lambda bi, i: (bi, nb - 1 - i, 0)), main(GROUP)],
        out_shape=[jax.ShapeDtypeStruct((b, s, GROUP), F32)] * 2 + [jax.ShapeDtypeStruct((b, s, GROUP), BF16)],
        scratch_shapes=_hgrn_scratch(tile) + [pltpu.VMEM((2, SUB_ROWS, s), BF16)],
        compiler_params=_params("parallel", "arbitrary"),
        name="mixers",
    )(gamma, ub, ub, ub, ub, ub, ub, q, k, v, gbd)


def _channel_kernel(x_ref, ya_ref, of_ref, ob_ref, g_ref, yc_ref, yd_ref, on_ref, gbb_ref, wo_ref,
                    gxq_ref, wq_ref, qn_ref, kx_ref, vx_ref, xwo_ref, gf_ref, w13_ref, w2_ref, o_ref, *, rows):
    tm = x_ref.shape[1]
    dff = w2_ref.shape[1]
    ones_bd = jnp.where(_head_ones(GROUP, HDIM), 1.0, 0.0).astype(BF16)
    lane = lax.broadcasted_iota(jnp.int32, (rows, GROUP), 1) // HDIM

    def mix(r0):
        sl = slice(r0, r0 + rows)
        o = of_ref[0, sl, :] + ob_ref[0, sl, :]
        g = g_ref[0, sl, :]
        yb = o * _head_rsqrt(o, ones_bd, HDIM) * on_ref[...] * (g * _sigmoid(g))
        yb = _rms(yb, gbb_ref[...]).astype(BF16)
        x1 = x_ref[0, sl, :]
        for j, y in enumerate((ya_ref[0, sl, :], yb, yc_ref[0, sl, :], yd_ref[0, sl, :])):
            x1 = x1 + _dot(y, wo_ref[0, j * GROUP:(j + 1) * GROUP, :])
        q = _dot(_rms(x1, gxq_ref[...]).astype(BF16), wq_ref[0])
        qn = q * _head_rsqrt(q, ones_bd, HDIM) * (qn_ref[...] * HDIM ** -0.5)
        acc = jnp.zeros((rows, GROUP), F32)
        for h in range(HEADS):
            s = _dot_nt(jnp.where(lane == h, qn, 0.0).astype(BF16), kx_ref[0, 0])
            p = jnp.exp(s - jnp.max(s, axis=-1, keepdims=True))
            l = jnp.sum(p, axis=-1, keepdims=True)
            pv = _dot(p.astype(BF16), vx_ref[0, 0])
            acc = jnp.where(lane == h, pv / l, acc)
        return x1 + _dot(acc.astype(BF16), xwo_ref[0])

    def ffn(x):
        a = _dot(_rms(x, gf_ref[...]).astype(BF16), w13_ref[0])
        a1 = a[:, :dff]
        return x + _dot((a1 * _sigmoid(a1) * a[:, dff:]).astype(BF16), w2_ref[0])

    starts = range(0, tm, rows)
    mixed = [mix(r0) for r0 in starts]
    for r0, x in zip(starts, mixed):
        o_ref[0, r0:r0 + rows, :] = ffn(x)


def _channel(x, ya, of, ob, ub, yc, yd, on, gbb, wo, gxq, wq, qn, kx, vx, xwo, gf, w13, w2, layer, tm, rows):
    b, s, d = x.shape
    n = kx.shape[2]
    row = lambda w, j=0: pl.BlockSpec((1, tm, w), lambda bi, i: (bi, i, j))
    of_layer = lambda a: pl.BlockSpec((1,) + a.shape[1:], lambda bi, i: (layer,) + (0,) * (a.ndim - 1),
                                      pipeline_mode=pl.Buffered(1))
    return pl.pallas_call(
        functools.partial(_channel_kernel, rows=rows),
        grid=(b, s // tm),
        in_specs=[row(d), row(GROUP), row(GROUP), row(GROUP), row(GROUP, 4), row(GROUP), row(GROUP),
                  _full(on.shape), _full(gbb.shape), of_layer(wo), _full(gxq.shape), of_layer(wq),
                  _full(qn.shape),
                  pl.BlockSpec((1, 1, n, GROUP), lambda bi, i: (layer, bi, 0, 0)),
                  pl.BlockSpec((1, 1, n, GROUP), lambda bi, i: (layer, bi, 0, 0)),
                  of_layer(xwo), _full(gf.shape), of_layer(w13), of_layer(w2)],
        out_specs=row(d),
        out_shape=jax.ShapeDtypeStruct((b, s, d), F32),
        compiler_params=_params("parallel", "parallel"),
        name="channel",
    )(x, ya, of, ob, ub, yc, yd, on, gbb, wo, gxq, wq, qn, kx, vx, xwo, gf, w13, w2)


def _pad_heads(w, width):
    lead = w.shape[:-1]
    w = w.reshape(lead + (HEADS, width))
    w = jnp.pad(w, [(0, 0)] * len(lead) + [(0, 0), (0, HPAD - width)])
    return w.reshape(lead + (HEADS * HPAD,))


def _swap_rope(w):
    half = ROPE // 2
    z = jnp.zeros_like(w[..., :HDIM])
    return jnp.concatenate([z, w[..., HDIM + half:], w[..., HDIM:HDIM + half]], axis=-1)


def _mla_weights(wuq, wukv, qn_g, kn_g):
    dq = wuq.shape[0]
    dkv = wukv.shape[0]
    wq3 = wuq.reshape(dq, HEADS, QK)
    wq = jnp.concatenate([_pad_heads(wuq, QK), _pad_heads(_swap_rope(wq3).reshape(dq, -1), QK)], axis=1)
    kv3 = wukv.reshape(dkv, HEADS, 2 * HDIM)
    k_nope = jnp.pad(kv3[..., :HDIM], [(0, 0), (0, 0), (0, HPAD - HDIM)]).reshape(dkv, -1)
    v_cols = kv3[..., HDIM:].reshape(dkv, -1)
    top = jnp.concatenate([k_nope, jnp.zeros_like(k_nope), v_cols], axis=1)
    eye = jnp.eye(ROPE, dtype=F32)
    place = jnp.pad(eye, [(0, 0), (HDIM, HPAD - QK)])
    place_sw = jnp.pad(jnp.roll(eye, ROPE // 2, axis=1), [(0, 0), (HDIM, HPAD - QK)])
    rope_rows = jnp.concatenate([jnp.tile(place, (1, HEADS)), jnp.tile(place_sw, (1, HEADS)),
                                 jnp.zeros((ROPE, v_cols.shape[1]), F32)], axis=1)
    rope_rows = jnp.pad(rope_rows, [(0, HPAD - ROPE), (0, 0)])
    wk = jnp.concatenate([top, rope_rows, rope_rows], axis=0)
    pad1 = lambda g: jnp.pad(g, (0, HPAD - QK))[None, :]
    return (wq.astype(BF16), wk.astype(BF16),
            pad1(qn_g), pad1(_swap_rope(qn_g)), pad1(kn_g), pad1(_swap_rope(kn_g)))


def kernel(x, mem, positions, g_mix, w_in, a_dw_w, a_dw_b, a_ln_g, a_ln_b, h_gamma, h_onorm_g, c_dw_w, c_dw_b, m_qa_g, m_wuq, m_kva_g, m_wukv, m_qn_g, m_kn_g, g_branch, w_out, g_xq, g_mem, x_wq, x_wkv, x_qn_g, x_kn_g, x_wo, g_ffn, f_w13, f_w2):
    b, s, d = x.shape
    t = b * s
    depth = w_in.shape[0]
    row = lambda v: v[None, :]
    tile4 = lambda v: jnp.tile(v, HEADS)[None, :]

    inv = ROPE_BASE ** (-jnp.arange(0, ROPE, 2, dtype=F32) / ROPE)
    inv_row = jnp.concatenate([jnp.zeros((HDIM,), F32), inv, inv, jnp.zeros((HPAD - QK,), F32)])[None, :]
    ct, sg = _rope_tables(positions.astype(F32).reshape(t, 1), inv_row, TOKEN_TILE)

    kv4 = x_wkv.reshape(depth, d, HEADS, 2, HDIM).transpose(0, 1, 3, 2, 4).reshape(depth, d, 2 * GROUP)
    kx, vx = _mem_kv(mem, g_mem[:, None, :], kv4.astype(BF16), jnp.tile(x_kn_g, (1, HEADS))[:, None, :])

    wo_all, xwq_all, xwo_all = w_out.astype(BF16), x_wq.astype(BF16), x_wo.astype(BF16)
    w13_all, w2_all = f_w13.astype(BF16), f_w2.astype(BF16)
    widths = (2 * GROUP, 5 * GROUP, 3 * GROUP, 2 * GROUP)
    w_in_all = jnp.pad(w_in, [(0, 0), (0, 0), (0, sum(widths) - w_in.shape[2])]).astype(BF16)
    for l in range(depth):
        mla = (row(m_qa_g[l]), row(m_kva_g[l])) + _mla_weights(m_wuq[l], m_wukv[l], m_qn_g[l], m_kn_g[l])
        gb = g_branch[l]
        conv = (a_dw_w[l], row(a_dw_b[l]), row(a_ln_g[l]), row(a_ln_b[l]), c_dw_w[l], row(c_dw_b[l]),
                row(gb[:GROUP]), row(gb[2 * GROUP:3 * GROUP]))
        ub, qr, kr, vr, ya, yc = _in_proj(x.reshape(t, d), row(g_mix[l]), w_in_all, ct, sg, mla, conv,
                                          widths, l, TOKEN_TILE, s)
        ub = ub.reshape(b, s, -1)
        of, ob, yd = _mixers(ub, qr.reshape(b, s, -1), kr.reshape(b, s, -1), vr.reshape(HEADS, b, s, HPAD),
                             h_gamma, row(gb[3 * GROUP:]), l, TOKEN_TILE)
        ya = ya.reshape(b, s, -1)
        yc = yc.reshape(b, s, -1)
        x = _channel(x, ya, of, ob, ub, yc, yd, row(h_onorm_g[l]), row(gb[GROUP:2 * GROUP]), wo_all,
                     row(g_xq[l]), xwq_all, tile4(x_qn_g[l]), kx, vx, xwo_all, row(g_ffn[l]), w13_all, w2_all,
                     l, TOKEN_TILE, SUB_ROWS)
    return x
```

```python
import functools

import jax
import jax.numpy as jnp
from jax import lax
from jax.experimental import pallas as pl
from jax.experimental.pallas import tpu as pltpu

F32 = jnp.float32
BF16 = jnp.bfloat16
EPS = 1e-6

GROUP = 256
HEADS = 4
HDIM = 64
ROPE = 32
QK = HDIM + ROPE
HPAD = 128
CONF_W = 31
SC_W = 3
HALO = 16
STEP = 16
SUBLANES = 8
ROPE_BASE = 10000.0
LOG2E = 1.4426950408889634
VMEM_LIMIT = 60 * 1024 * 1024

TOKEN_TILE = 512
SUB_ROWS = 256
ATTN_ROWS = 512
KEY_CHUNK = 512
CONV_ROWS = 64


def _rms(x, g):
    return x * lax.rsqrt(jnp.mean(x * x, axis=-1, keepdims=True) + EPS) * g


def _sigmoid(x):
    return 1.0 / (1.0 + jnp.exp(-x))


def _dot(a, b):
    return jnp.dot(a, b, preferred_element_type=F32)


def _dot_nt(a, b):
    return lax.dot_general(a, b, (((1,), (1,)), ((), ())), preferred_element_type=F32)


def _dot_tn(a, b):
    return lax.dot_general(a, b, (((0,), (0,)), ((), ())), preferred_element_type=F32)


def _split_dot(m, x):
    hi = x.astype(BF16)
    lo = (x - hi.astype(F32)).astype(BF16)
    return _dot(m, hi) + _dot(m, lo)


def _head_ones(n, width):
    r = lax.broadcasted_iota(jnp.int32, (n, n), 0) // width
    c = lax.broadcasted_iota(jnp.int32, (n, n), 1) // width
    return r == c


def _head_rsqrt(x, ones_bd, width):
    return lax.rsqrt(_split_dot_rhs(x * x, ones_bd) * (1.0 / width) + EPS)


def _split_dot_rhs(x, m):
    hi = x.astype(BF16)
    lo = (x - hi.astype(F32)).astype(BF16)
    return _dot(hi, m) + _dot(lo, m)


def _params(*sem):
    return pltpu.CompilerParams(dimension_semantics=sem, vmem_limit_bytes=VMEM_LIMIT)


def _full(shape):
    nd = len(shape)
    return pl.BlockSpec(shape, lambda *_: (0,) * nd)


def _rope_kernel(pos_ref, inv_ref, ct_ref, sg_ref):
    ang = pos_ref[...] * inv_ref[...]
    lane = lax.broadcasted_iota(jnp.int32, ang.shape, 1)
    c = jnp.cos(ang)
    s = jnp.sin(ang)
    half = ROPE // 2
    ct_ref[...] = jnp.where(lane < HDIM, 1.0, jnp.where(lane < QK, c, 0.0))
    sg_ref[...] = jnp.where(lane < HDIM, 0.0,
                            jnp.where(lane < HDIM + half, -s, jnp.where(lane < QK, s, 0.0)))


def _rope_tables(pos_f, inv_row, tm):
    t = pos_f.shape[0]
    return pl.pallas_call(
        _rope_kernel,
        grid=(t // tm,),
        in_specs=[pl.BlockSpec((tm, 1), lambda i: (i, 0)), _full((1, HPAD))],
        out_specs=[pl.BlockSpec((tm, HPAD), lambda i: (i, 0))] * 2,
        out_shape=[jax.ShapeDtypeStruct((t, HPAD), F32)] * 2,
        compiler_params=_params("parallel"),
        name="rope_tables",
    )(pos_f, inv_row)


def _memkv_kernel(mem_ref, g_ref, w_ref, kn_ref, k_ref, v_ref):
    mn = _rms(mem_ref[0], g_ref[0]).astype(BF16)
    kv = _dot(mn, w_ref[0])
    k = kv[:, :GROUP]
    ones_bd = jnp.where(_head_ones(GROUP, HDIM), 1.0, 0.0).astype(BF16)
    k_ref[0, 0] = (k * _head_rsqrt(k, ones_bd, HDIM) * kn_ref[0]).astype(BF16)
    v_ref[0, 0] = kv[:, GROUP:].astype(BF16)


def _mem_kv(mem, g_mem, wkv, kn_g):
    b, n, d = mem.shape
    nl = wkv.shape[0]
    return pl.pallas_call(
        _memkv_kernel,
        grid=(nl, b),
        in_specs=[pl.BlockSpec((1, n, d), lambda l, i: (i, 0, 0)),
                  pl.BlockSpec((1, 1, d), lambda l, i: (l, 0, 0)),
                  pl.BlockSpec((1, d, 2 * GROUP), lambda l, i: (l, 0, 0)),
                  pl.BlockSpec((1, 1, GROUP), lambda l, i: (l, 0, 0))],
        out_specs=[pl.BlockSpec((1, 1, n, GROUP), lambda l, i: (l, i, 0, 0))] * 2,
        out_shape=[jax.ShapeDtypeStruct((nl, b, n, GROUP), BF16)] * 2,
        compiler_params=_params("parallel", "parallel"),
        name="mem_kv",
    )(mem, g_mem, wkv, kn_g)


def _inproj_kernel(x_ref, xp_ref, xn_ref, g_ref, w_ref, ct_ref, sg_ref, qag_ref, kvag_ref, wq_ref, wk_ref,
                   gq_ref, gqs_ref, gk_ref, gks_ref, aw_ref, ab_ref, lng_ref, lnb_ref, cw_ref, cb_ref, gba_ref, gbc_ref,
                   ub_ref, q_ref, k_ref, v_ref, ya_ref, yc_ref, ha_scr, hc_scr, gate_scr,
                   *, widths, tiles_per_seq, conv_rows):
    wa, wb, wc, wd = widths
    i = pl.program_id(0) % tiles_per_seq
    norm = lambda x: _rms(x, g_ref[...]).astype(BF16)
    n = norm(x_ref[...])
    _mla_qkv(_dot(n, w_ref[0, :, wa + wb + wc:wa + wb + wc + wd]), ct_ref, sg_ref, qag_ref, kvag_ref, wq_ref, wk_ref,
             gq_ref, gqs_ref, gk_ref, gks_ref, q_ref, k_ref, v_ref)
    n_ext = jnp.concatenate([norm(xp_ref[...]), n, norm(xn_ref[...])], axis=0)
    conv_fill, conv_tile = _conv_parts(aw_ref, ab_ref, lng_ref, lnb_ref, cw_ref, cb_ref, gba_ref, gbc_ref,
                                       ya_ref, yc_ref, ha_scr, hc_scr, gate_scr, conv_rows)
    conv_fill(_dot(n_ext, w_ref[0, :, 0:wa]), _dot(n_ext, w_ref[0, :, wa + wb:wa + wb + wc]),
              jnp.where(i > 0, 1.0, 0.0), jnp.where(i < tiles_per_seq - 1, 1.0, 0.0))
    ub_ref[...] = _dot(n, w_ref[0, :, wa:wa + wb])
    for r0 in range(0, x_ref.shape[0], conv_rows):
        conv_tile(r0)


def _in_proj(x2, g, w, ct, sg, mla, conv, widths, layer, tm, seq):
    t, d = x2.shape
    nw = HEADS * HPAD
    hb = tm // HALO
    last = t // HALO - 1
    row = lambda n: pl.BlockSpec((tm, n), lambda i: (i, 0))
    return pl.pallas_call(
        functools.partial(_inproj_kernel, widths=widths, tiles_per_seq=seq // tm, conv_rows=CONV_ROWS),
        grid=(t // tm,),
        in_specs=[row(d),
                  pl.BlockSpec((HALO, d), lambda i: (jnp.maximum(i * hb - 1, 0), 0)),
                  pl.BlockSpec((HALO, d), lambda i: (jnp.minimum((i + 1) * hb, last), 0)),
                  _full((1, d)),
                  pl.BlockSpec((1,) + w.shape[1:], lambda i: (layer, 0, 0), pipeline_mode=pl.Buffered(1)),
                  row(HPAD), row(HPAD)] + [_full(a.shape) for a in mla + conv],
        out_specs=[row(widths[1]), row(nw), row(nw), pl.BlockSpec((HEADS, tm, HPAD), lambda i: (0, i, 0)),
                   row(GROUP), row(GROUP)],
        out_shape=[jax.ShapeDtypeStruct((t, widths[1]), F32),
                   jax.ShapeDtypeStruct((t, nw), BF16), jax.ShapeDtypeStruct((t, nw), BF16),
                   jax.ShapeDtypeStruct((HEADS, t, HPAD), BF16),
                   jax.ShapeDtypeStruct((t, GROUP), BF16), jax.ShapeDtypeStruct((t, GROUP), BF16)],
        scratch_shapes=[pltpu.VMEM((tm + 2 * HALO, GROUP), F32)] * 2 + [pltpu.VMEM((tm, GROUP), F32)],
        compiler_params=_params("parallel"),
        name="in_proj",
    )(x2, x2, x2, g, w, ct, sg, *mla, *conv)


def _conv_parts(aw_ref, ab_ref, lng_ref, lnb_ref, cw_ref, cb_ref, gba_ref, gbc_ref,
                ya_ref, yc_ref, ha_scr, hc_scr, gate_scr, sub):
    tc = gate_scr.shape[0]
    pad_a = (CONF_W - 1) // 2
    pad_c = (SC_W - 1) // 2

    def fill(ua, uc, has_prev, has_next):
        ha = ua[:, :GROUP] * _sigmoid(ua[:, GROUP:])
        hc = uc[:, GROUP:2 * GROUP] * uc[:, 2 * GROUP:]
        for scr, h in ((ha_scr, ha), (hc_scr, hc)):
            scr[0:HALO, :] = h[0:HALO] * has_prev
            scr[HALO:HALO + tc, :] = h[HALO:HALO + tc]
            scr[HALO + tc:, :] = h[HALO + tc:] * has_next
        gate_scr[...] = uc[HALO:HALO + tc, :GROUP]

    def tile(r0):
        acc = jnp.broadcast_to(ab_ref[...], (sub, GROUP))
        for r in range(SUBLANES):
            part = None
            for j in range(CONF_W):
                start = HALO - pad_a + j
                if start % SUBLANES != r:
                    continue
                lo = r0 + start - r
                term = aw_ref[j:j + 1, :] * ha_scr[lo:lo + sub + SUBLANES, :]
                part = term if part is None else part + term
            if part is not None:
                acc = acc + part[r:r + sub, :]
        mu = jnp.mean(acc, axis=-1, keepdims=True)
        xc = acc - mu
        y = xc * lax.rsqrt(jnp.mean(xc * xc, axis=-1, keepdims=True) + EPS) * lng_ref[...] + lnb_ref[...]
        y = y * _sigmoid(y)
        ya_ref[r0:r0 + sub, :] = _rms(y, gba_ref[...]).astype(BF16)

        acc = jnp.broadcast_to(cb_ref[...], (sub, GROUP))
        for j in range(SC_W):
            acc = acc + cw_ref[j:j + 1, :] * hc_scr[r0 + HALO - pad_c + j:r0 + HALO - pad_c + j + sub, :]
        y = gate_scr[r0:r0 + sub, :] * acc
        yc_ref[r0:r0 + sub, :] = _rms(y, gbc_ref[...]).astype(BF16)

    return fill, tile


def _hgrn_parts(gam_ref, qf_ref, zf_ref, vf_ref, qb_ref, zb_ref, vb_ref, of_ref, ob_ref,
                st_f, st_b, b_f, b_b, c_f, c_b, lam_f, lam_b, qe_f, qe_b, kt_f, kt_b,
                sts_f, sts_b, ones_scr, layer):
    tb = qf_ref.shape[1]
    nst = tb // STEP
    half = STEP // 2
    cs = 256

    ones_scr[...] = jnp.where(_head_ones(GROUP, HDIM), 1.0, 0.0).astype(BF16)

    ti = lax.broadcasted_iota(jnp.int32, (cs, cs), 0)
    si = lax.broadcasted_iota(jnp.int32, (cs, cs), 1)
    same = (ti // STEP) == (si // STEP)
    m_all = jnp.where(same, 1.0, 0.0).astype(BF16)

    def prep(d, q_ref, z_ref, b_scr, c_scr, lam_scr, qe_scr, kt_scr):
        g = gam_ref[d]
        e = jnp.exp(g - jnp.max(g, axis=0, keepdims=True))
        tot = jnp.sum(e, axis=0, keepdims=True)
        if layer == 0:
            lb = jnp.zeros_like(tot)
        else:
            lb = jnp.sum(e[1:layer + 1], axis=0, keepdims=True) / tot
        tri = (si <= ti) if d == 0 else (si >= ti)
        m_tri = jnp.where(same & tri, 1.0, 0.0).astype(BF16)
        for r0 in range(0, tb, cs):
            z = z_ref[0, r0:r0 + cs, :]
            a = jnp.exp(-jnp.abs(z))
            big = 1.0 / (1.0 + a)
            pos = z >= 0.0
            f = lb + (1.0 - lb) * jnp.where(pos, big, a * big)
            kk = (1.0 - lb) * jnp.where(pos, a * big, big)
            lf = jnp.log2(f)
            bloc = _split_dot(m_tri, lf)
            btot = _split_dot(m_all, lf)
            b_scr[r0:r0 + cs, :] = bloc
            c_scr[r0:r0 + cs, :] = bloc - jnp.log2(kk)
            lam_scr[r0:r0 + cs, :] = jnp.exp2(btot)
            qe_scr[r0:r0 + cs, :] = (q_ref[0, r0:r0 + cs, :] * jnp.exp2(bloc)).astype(BF16)
            kt_scr[r0:r0 + cs, :] = (kk * jnp.exp2(btot - bloc)).astype(BF16)

    prep(0, qf_ref, zf_ref, b_f, c_f, lam_f, qe_f, kt_f)
    prep(1, qb_ref, zb_ref, b_b, c_b, lam_b, qe_b, kt_b)

    trow = lax.broadcasted_iota(jnp.int32, (half, GROUP), 0)
    head_of_lane = lax.broadcasted_iota(jnp.int32, (STEP, GROUP), 1) // HDIM

    def liveness(s, rev):
        if rev:
            return (1, 0) if s < half else (2, 1)
        return (1, 2) if s < half else (0, 1)

    def per_head_rows(x):
        zero = jnp.zeros_like(x)
        return jnp.concatenate([jnp.where(head_of_lane == h, x, zero) for h in range(HEADS)], axis=0)

    def advance(k, r, state, v_ref, kt_scr, lam_scr, sts_scr):
        vb = v_ref[0, pl.ds(r, STEP), :].astype(BF16)
        vstack = jnp.concatenate([vb[:, h * HDIM:(h + 1) * HDIM] for h in range(HEADS)], axis=0)
        upd = _dot_tn(vstack, per_head_rows(kt_scr[pl.ds(r, STEP), :]))
        sts_scr[pl.ds(k * HDIM, HDIM), :] = state.astype(BF16)
        return lam_scr[pl.ds(r, 1), :] * state + upd

    def output(k, r, rev, q_ref, v_ref, o_ref, b_scr, c_scr, qe_scr, sts_scr):
        res = _dot_nt(per_head_rows(qe_scr[pl.ds(r, STEP), :]), sts_scr[pl.ds(k * HDIM, HDIM), :])
        o = jnp.concatenate([res[h * STEP:(h + 1) * STEP, :] for h in range(HEADS)], axis=1)
        o = [o[:half], o[half:]]
        q = (q_ref[0, pl.ds(r, half), :], q_ref[0, pl.ds(r + half, half), :])
        b = (b_scr[pl.ds(r, half), :], b_scr[pl.ds(r + half, half), :])
        pieces, where_to = [], []
        for s in range(STEP):
            crow = jnp.broadcast_to(c_scr[pl.ds(r + s, 1), :], (half, GROUP))
            for hx, kind in enumerate(liveness(s, rev)):
                if kind == 0:
                    continue
                dec = q[hx] * jnp.exp2(b[hx] - crow)
                if kind == 1:
                    sl = s - hx * half
                    dec = jnp.where((trow <= sl) if rev else (trow >= sl), dec, 0.0)
                pieces.append(dec)
                where_to.append((s, hx))
        att = _dot(jnp.concatenate(pieces, axis=0).astype(BF16), ones_scr[...])
        vrows = [jnp.broadcast_to(v_ref[0, pl.ds(r + s, 1), :], (half, GROUP)) for s in range(STEP)]
        for n, (s, hx) in enumerate(where_to):
            o[hx] = o[hx] + att[n * half:(n + 1) * half, :] * vrows[s]
        o_ref[0, pl.ds(r, half), :] = o[0]
        o_ref[0, pl.ds(r + half, half), :] = o[1]

    states = [st_f[...], st_b[...]]

    def state_step(k):
        states[0] = advance(k, k * STEP, states[0], vf_ref, kt_f, lam_f, sts_f)
        states[1] = advance(k, (nst - 1 - k) * STEP, states[1], vb_ref, kt_b, lam_b, sts_b)
        if k == nst - 1:
            st_f[...] = states[0]
            st_b[...] = states[1]

    def output_step(k):
        output(k, k * STEP, False, qf_ref, vf_ref, of_ref, b_f, c_f, qe_f, sts_f)
        output(k, (nst - 1 - k) * STEP, True, qb_ref, vb_ref, ob_ref, b_b, c_b, qe_b, sts_b)

    return nst, state_step, output_step


def _hgrn_scratch(tb):
    vm = lambda dt: pltpu.VMEM((tb, GROUP), dt)
    return ([pltpu.VMEM((HDIM, GROUP), F32)] * 2 + [vm(F32)] * 6 + [vm(BF16)] * 4
            + [pltpu.VMEM((tb // STEP * HDIM, GROUP), BF16)] * 2
            + [pltpu.VMEM((GROUP, GROUP), BF16)])


def _mla_qkv(ud, ct_ref, sg_ref, qag_ref, kvag_ref, wq_ref, wk_ref,
             gq_ref, gqs_ref, gk_ref, gks_ref, q_ref, k_ref, v_ref):
    nw = HEADS * HPAD
    nq = _rms(ud[:, :GROUP], qag_ref[...]).astype(BF16)
    qq = _dot(nq, wq_ref[...])
    nkv = _rms(ud[:, GROUP:GROUP + HPAD], kvag_ref[...]).astype(BF16)
    kr = ud[:, GROUP + HPAD:]
    kr_hi = kr.astype(BF16)
    kr_lo = (kr - kr_hi.astype(F32)).astype(BF16)
    kk = _dot(jnp.concatenate([nkv, kr_hi, kr_lo], axis=-1), wk_ref[...])
    ct = ct_ref[...]
    sg = sg_ref[...]
    scale = QK ** -0.5 * LOG2E
    for h in range(HEADS):
        sl = slice(h * HPAD, (h + 1) * HPAD)
        sw = slice(nw + h * HPAD, nw + (h + 1) * HPAD)
        x = qq[:, sl]
        r = lax.rsqrt(jnp.sum(x * x, axis=-1, keepdims=True) * (1.0 / QK) + EPS) * scale
        q_ref[:, sl] = (r * (x * (gq_ref[...] * ct) + qq[:, sw] * (gqs_ref[...] * sg))).astype(BF16)
        x = kk[:, sl]
        r = lax.rsqrt(jnp.sum(x * x, axis=-1, keepdims=True) * (1.0 / QK) + EPS)
        k_ref[:, sl] = (r * (x * (gk_ref[...] * ct) + kk[:, sw] * (gks_ref[...] * sg))).astype(BF16)
    first = lax.broadcasted_iota(jnp.int32, (ud.shape[0], HPAD), 1) < HDIM
    for pair in range(HEADS // 2):
        vv = kk[:, 2 * nw + pair * HPAD:2 * nw + (pair + 1) * HPAD]
        v_ref[2 * pair] = jnp.where(first, vv, 1.0).astype(BF16)
        v_ref[2 * pair + 1] = jnp.where(first, 1.0, vv).astype(BF16)


def _attn_parts(q_ref, k_ref, v_ref, gb_ref, o_ref, sb_scr, rows, ck):
    tq = q_ref.shape[1]
    n_keys = k_ref.shape[1]
    first = lax.broadcasted_iota(jnp.int32, (rows, HPAD), 1) < HDIM
    items = [(r0, h) for r0 in range(0, tq, rows) for h in range(HEADS)]
    done = []

    def scores(n):
        r0, h = items[n]
        sl = slice(h * HPAD, (h + 1) * HPAD)
        q = q_ref[0, r0:r0 + rows, sl]
        top = None
        for c in range(0, n_keys, ck):
            sb = _dot_nt(q, k_ref[0, c:c + ck, sl]).astype(BF16)
            sb_scr[n % 2, :, c:c + ck] = sb
            for j in range(0, ck, HPAD):
                top = sb[:, j:j + HPAD] if top is None else jnp.maximum(top, sb[:, j:j + HPAD])
        return jnp.max(top, axis=-1, keepdims=True)

    def finish(n, m):
        r0, h = items[n]
        w = None
        for c in range(0, n_keys, ck):
            p = jnp.exp2(sb_scr[n % 2, :, c:c + ck] - m)
            part = _dot(p, v_ref[h, 0, c:c + ck, :])
            w = part if w is None else w + part
        w = w / pltpu.roll(w, HDIM, axis=1)
        done.append(w)
        if h == HEADS - 1:
            pairs = [jnp.where(first, done[2 * j], done[2 * j + 1]) for j in range(HEADS // 2)]
            del done[:]
            o_ref[0, r0:r0 + rows, :] = _rms(jnp.concatenate(pairs, axis=1), gb_ref[...]).astype(BF16)

    return items, scores, finish


N_HGRN_SCRATCH = 15


def _mixers_kernel(*refs, layer, rows, key_chunk):
    (gam_ref, qf_ref, zf_ref, vf_ref, qb_ref, zb_ref, vb_ref, q_ref, k_ref, v_ref, gbd_ref,
     of_ref, ob_ref, yd_ref) = refs[:14]
    hgrn_scr = refs[14:14 + N_HGRN_SCRATCH]
    sb_scr, = refs[14 + N_HGRN_SCRATCH:]

    @pl.when(pl.program_id(1) == 0)
    def _():
        for st in hgrn_scr[:2]:
            st[...] = jnp.zeros_like(st)

    nst, state_step, output_step = _hgrn_parts(gam_ref, qf_ref, zf_ref, vf_ref, qb_ref, zb_ref, vb_ref,
                                               of_ref, ob_ref, *hgrn_scr, layer)
    items, scores, finish = _attn_parts(q_ref, k_ref, v_ref, gbd_ref, yd_ref, sb_scr, rows, key_chunk)

    def share(seq, n, first, count):
        n -= first
        return seq[n * len(seq) // count:(n + 1) * len(seq) // count] if 0 <= n < count else []

    state_items = len(items) // 2
    first_out = len(items) // 4
    m_next = scores(0)
    for n in range(len(items)):
        m = m_next
        if n + 1 < len(items):
            m_next = scores(n + 1)
        finish(n, m)
        for k in share(range(nst), n, 0, state_items):
            state_step(k)
        for k in share(range(nst), n, first_out, len(items) - first_out):
            output_step(k)


def _mixers(ub, q, k, v, gamma, gbd, layer, tile):
    b, s, _ = ub.shape
    nb = s // tile
    col = lambda j, rev: pl.BlockSpec(
        (1, tile, GROUP), (lambda bi, i: (bi, nb - 1 - i, j)) if rev else (lambda bi, i: (bi, i, j)))
    main = lambda w: pl.BlockSpec((1, tile, w), lambda bi, i: (bi, i, 0))
    nw = q.shape[2]
    return pl.pallas_call(
        functools.partial(_mixers_kernel, layer=layer, rows=ATTN_ROWS, key_chunk=KEY_CHUNK),
        grid=(b, nb),
        in_specs=[_full(gamma.shape),
                  col(0, False), col(1, False), col(3, False), col(0, True), col(2, True), col(3, True),
                  main(nw), pl.BlockSpec((1, s, nw), lambda bi, i: (bi, 0, 0)),
                  pl.BlockSpec((HEADS, 1, s, HPAD), lambda bi, i: (0, bi, 0, 0)),
                  _full(gbd.shape)],
        out_specs=[main(GROUP), pl.BlockSpec((1, tile, GROUP), lambda bi, i: (bi, nb - 1 - i, 0)), main(GROUP)],
        out_shape=[jax.ShapeDtypeStruct((b, s, GROUP), F32)] * 2 + [jax.ShapeDtypeStruct((b, s, GROUP), BF16)],
        scratch_shapes=_hgrn_scratch(tile) + [pltpu.VMEM((2, ATTN_ROWS, s), BF16)],
        compiler_params=_params("parallel", "arbitrary"),
        name="mixers",
    )(gamma, ub, ub, ub, ub, ub, ub, q, k, v, gbd)


def _channel_kernel(x_ref, ya_ref, of_ref, ob_ref, g_ref, yc_ref, yd_ref, on_ref, gbb_ref, wo_ref,
                    gxq_ref, wq_ref, qn_ref, kx_ref, vx_ref, xwo_ref, gf_ref, w13_ref, w2_ref, o_ref, *, rows):
    tm = x_ref.shape[1]
    dff = w2_ref.shape[1]
    ones_bd = jnp.where(_head_ones(GROUP, HDIM), 1.0, 0.0).astype(BF16)
    lane = lax.broadcasted_iota(jnp.int32, (rows, GROUP), 1) // HDIM

    def mix(r0):
        sl = slice(r0, r0 + rows)
        o = of_ref[0, sl, :] + ob_ref[0, sl, :]
        g = g_ref[0, sl, :]
        yb = o * _head_rsqrt(o, ones_bd, HDIM) * on_ref[...] * (g * _sigmoid(g))
        yb = _rms(yb, gbb_ref[...]).astype(BF16)
        x1 = x_ref[0, sl, :]
        for j, y in enumerate((ya_ref[0, sl, :], yb, yc_ref[0, sl, :], yd_ref[0, sl, :])):
            x1 = x1 + _dot(y, wo_ref[0, j * GROUP:(j + 1) * GROUP, :])
        q = _dot(_rms(x1, gxq_ref[...]).astype(BF16), wq_ref[0])
        qn = q * _head_rsqrt(q, ones_bd, HDIM) * (qn_ref[...] * HDIM ** -0.5)
        acc = jnp.zeros((rows, GROUP), F32)
        for h in range(HEADS):
            s = _dot_nt(jnp.where(lane == h, qn, 0.0).astype(BF16), kx_ref[0, 0])
            p = jnp.exp(s - jnp.max(s, axis=-1, keepdims=True))
            l = jnp.sum(p, axis=-1, keepdims=True)
            pv = _dot(p.astype(BF16), vx_ref[0, 0])
            acc = jnp.where(lane == h, pv / l, acc)
        return x1 + _dot(acc.astype(BF16), xwo_ref[0])

    def ffn(x):
        a = _dot(_rms(x, gf_ref[...]).astype(BF16), w13_ref[0])
        a1 = a[:, :dff]
        return x + _dot((a1 * _sigmoid(a1) * a[:, dff:]).astype(BF16), w2_ref[0])

    starts = range(0, tm, rows)
    mixed = [mix(r0) for r0 in starts]
    for r0, x in zip(starts, mixed):
        o_ref[0, r0:r0 + rows, :] = ffn(x)


def _channel(x, ya, of, ob, ub, yc, yd, on, gbb, wo, gxq, wq, qn, kx, vx, xwo, gf, w13, w2, layer, tm, rows):
    b, s, d = x.shape
    n = kx.shape[2]
    row = lambda w, j=0: pl.BlockSpec((1, tm, w), lambda bi, i: (bi, i, j))
    of_layer = lambda a: pl.BlockSpec((1,) + a.shape[1:], lambda bi, i: (layer,) + (0,) * (a.ndim - 1),
                                      pipeline_mode=pl.Buffered(1))
    return pl.pallas_call(
        functools.partial(_channel_kernel, rows=rows),
        grid=(b, s // tm),
        in_specs=[row(d), row(GROUP), row(GROUP), row(GROUP), row(GROUP, 4), row(GROUP), row(GROUP),
                  _full(on.shape), _full(gbb.shape), of_layer(wo), _full(gxq.shape), of_layer(wq),
                  _full(qn.shape),
                  pl.BlockSpec((1, 1, n, GROUP), lambda bi, i: (layer, bi, 0, 0)),
                  pl.BlockSpec((1, 1, n, GROUP), lambda bi, i: (layer, bi, 0, 0)),
                  of_layer(xwo), _full(gf.shape), of_layer(w13), of_layer(w2)],
        out_specs=row(d),
        out_shape=jax.ShapeDtypeStruct((b, s, d), F32),
        compiler_params=_params("parallel", "parallel"),
        name="channel",
    )(x, ya, of, ob, ub, yc, yd, on, gbb, wo, gxq, wq, qn, kx, vx, xwo, gf, w13, w2)


def _pad_heads(w, width):
    lead = w.shape[:-1]
    w = w.reshape(lead + (HEADS, width))
    w = jnp.pad(w, [(0, 0)] * len(lead) + [(0, 0), (0, HPAD - width)])
    return w.reshape(lead + (HEADS * HPAD,))


def _swap_rope(w):
    half = ROPE // 2
    z = jnp.zeros_like(w[..., :HDIM])
    return jnp.concatenate([z, w[..., HDIM + half:], w[..., HDIM:HDIM + half]], axis=-1)


def _mla_weights(wuq, wukv, qn_g, kn_g):
    dq = wuq.shape[0]
    dkv = wukv.shape[0]
    wq3 = wuq.reshape(dq, HEADS, QK)
    wq = jnp.concatenate([_pad_heads(wuq, QK), _pad_heads(_swap_rope(wq3).reshape(dq, -1), QK)], axis=1)
    kv3 = wukv.reshape(dkv, HEADS, 2 * HDIM)
    k_nope = jnp.pad(kv3[..., :HDIM], [(0, 0), (0, 0), (0, HPAD - HDIM)]).reshape(dkv, -1)
    v_cols = kv3[..., HDIM:].reshape(dkv, -1)
    top = jnp.concatenate([k_nope, jnp.zeros_like(k_nope), v_cols], axis=1)
    eye = jnp.eye(ROPE, dtype=F32)
    place = jnp.pad(eye, [(0, 0), (HDIM, HPAD - QK)])
    place_sw = jnp.pad(jnp.roll(eye, ROPE // 2, axis=1), [(0, 0), (HDIM, HPAD - QK)])
    rope_rows = jnp.concatenate([jnp.tile(place, (1, HEADS)), jnp.tile(place_sw, (1, HEADS)),
                                 jnp.zeros((ROPE, v_cols.shape[1]), F32)], axis=1)
    rope_rows = jnp.pad(rope_rows, [(0, HPAD - ROPE), (0, 0)])
    wk = jnp.concatenate([top, rope_rows, rope_rows], axis=0)
    pad1 = lambda g: jnp.pad(g, (0, HPAD - QK))[None, :]
    return (wq.astype(BF16), wk.astype(BF16),
            pad1(qn_g), pad1(_swap_rope(qn_g)), pad1(kn_g), pad1(_swap_rope(kn_g)))


def kernel(x, mem, positions, g_mix, w_in, a_dw_w, a_dw_b, a_ln_g, a_ln_b, h_gamma, h_onorm_g, c_dw_w, c_dw_b, m_qa_g, m_wuq, m_kva_g, m_wukv, m_qn_g, m_kn_g, g_branch, w_out, g_xq, g_mem, x_wq, x_wkv, x_qn_g, x_kn_g, x_wo, g_ffn, f_w13, f_w2):
    b, s, d = x.shape
    t = b * s
    depth = w_in.shape[0]
    row = lambda v: v[None, :]
    tile4 = lambda v: jnp.tile(v, HEADS)[None, :]

    inv = ROPE_BASE ** (-jnp.arange(0, ROPE, 2, dtype=F32) / ROPE)
    inv_row = jnp.concatenate([jnp.zeros((HDIM,), F32), inv, inv, jnp.zeros((HPAD - QK,), F32)])[None, :]
    ct, sg = _rope_tables(positions.astype(F32).reshape(t, 1), inv_row, TOKEN_TILE)

    kv4 = x_wkv.reshape(depth, d, HEADS, 2, HDIM).transpose(0, 1, 3, 2, 4).reshape(depth, d, 2 * GROUP)
    kx, vx = _mem_kv(mem, g_mem[:, None, :], kv4.astype(BF16), jnp.tile(x_kn_g, (1, HEADS))[:, None, :])

    wo_all, xwq_all, xwo_all = w_out.astype(BF16), x_wq.astype(BF16), x_wo.astype(BF16)
    w13_all, w2_all = f_w13.astype(BF16), f_w2.astype(BF16)
    widths = (2 * GROUP, 5 * GROUP, 3 * GROUP, 2 * GROUP)
    w_in_all = jnp.pad(w_in, [(0, 0), (0, 0), (0, sum(widths) - w_in.shape[2])]).astype(BF16)
    for l in range(depth):
        mla = (row(m_qa_g[l]), row(m_kva_g[l])) + _mla_weights(m_wuq[l], m_wukv[l], m_qn_g[l], m_kn_g[l])
        gb = g_branch[l]
        conv = (a_dw_w[l], row(a_dw_b[l]), row(a_ln_g[l]), row(a_ln_b[l]), c_dw_w[l], row(c_dw_b[l]),
                row(gb[:GROUP]), row(gb[2 * GROUP:3 * GROUP]))
        ub, qr, kr, vr, ya, yc = _in_proj(x.reshape(t, d), row(g_mix[l]), w_in_all, ct, sg, mla, conv,
                                          widths, l, TOKEN_TILE, s)
        ub = ub.reshape(b, s, -1)
        of, ob, yd = _mixers(ub, qr.reshape(b, s, -1), kr.reshape(b, s, -1), vr.reshape(HEADS, b, s, HPAD),
                             h_gamma, row(gb[3 * GROUP:]), l, TOKEN_TILE)
        ya = ya.reshape(b, s, -1)
        yc = yc.reshape(b, s, -1)
        x = _channel(x, ya, of, ob, ub, yc, yd, row(h_onorm_g[l]), row(gb[GROUP:2 * GROUP]), wo_all,
                     row(g_xq[l]), xwq_all, tile4(x_qn_g[l]), kx, vx, xwo_all, row(g_ffn[l]), w13_all, w2_all,
                     l, TOKEN_TILE, SUB_ROWS)
    return x
```

```python
import functools

import jax
import jax.numpy as jnp
from jax import lax
from jax.experimental import pallas as pl
from jax.experimental.pallas import tpu as pltpu

F32 = jnp.float32
BF16 = jnp.bfloat16
EPS = 1e-6

GROUP = 256
HEADS = 4
HDIM = 64
ROPE = 32
QK = HDIM + ROPE
HPAD = 128
CONF_W = 31
SC_W = 3
HALO = 16
STEP = 16
SUBLANES = 8
ROPE_BASE = 10000.0
LOG2E = 1.4426950408889634
VMEM_LIMIT = 60 * 1024 * 1024

TOKEN_TILE = 512
SUB_ROWS = 256
KEY_CHUNK = 512
CONV_ROWS = 64


def _rms(x, g):
    return x * lax.rsqrt(jnp.mean(x * x, axis=-1, keepdims=True) + EPS) * g


def _sigmoid(x):
    return 1.0 / (1.0 + jnp.exp(-x))


def _dot(a, b):
    return jnp.dot(a, b, preferred_element_type=F32)


def _dot_nt(a, b):
    return lax.dot_general(a, b, (((1,), (1,)), ((), ())), preferred_element_type=F32)


def _dot_tn(a, b):
    return lax.dot_general(a, b, (((0,), (0,)), ((), ())), preferred_element_type=F32)


def _split_dot(m, x):
    hi = x.astype(BF16)
    lo = (x - hi.astype(F32)).astype(BF16)
    return _dot(m, hi) + _dot(m, lo)


def _head_ones(n, width):
    r = lax.broadcasted_iota(jnp.int32, (n, n), 0) // width
    c = lax.broadcasted_iota(jnp.int32, (n, n), 1) // width
    return r == c


def _head_rsqrt(x, ones_bd, width):
    return lax.rsqrt(_split_dot_rhs(x * x, ones_bd) * (1.0 / width) + EPS)


def _split_dot_rhs(x, m):
    hi = x.astype(BF16)
    lo = (x - hi.astype(F32)).astype(BF16)
    return _dot(hi, m) + _dot(lo, m)


def _params(*sem):
    return pltpu.CompilerParams(dimension_semantics=sem, vmem_limit_bytes=VMEM_LIMIT)


def _full(shape):
    nd = len(shape)
    return pl.BlockSpec(shape, lambda *_: (0,) * nd)


def _rope_kernel(pos_ref, inv_ref, ct_ref, sg_ref):
    ang = pos_ref[...] * inv_ref[...]
    lane = lax.broadcasted_iota(jnp.int32, ang.shape, 1)
    c = jnp.cos(ang)
    s = jnp.sin(ang)
    half = ROPE // 2
    ct_ref[...] = jnp.where(lane < HDIM, 1.0, jnp.where(lane < QK, c, 0.0))
    sg_ref[...] = jnp.where(lane < HDIM, 0.0,
                            jnp.where(lane < HDIM + half, -s, jnp.where(lane < QK, s, 0.0)))


def _rope_tables(pos_f, inv_row, tm):
    t = pos_f.shape[0]
    return pl.pallas_call(
        _rope_kernel,
        grid=(t // tm,),
        in_specs=[pl.BlockSpec((tm, 1), lambda i: (i, 0)), _full((1, HPAD))],
        out_specs=[pl.BlockSpec((tm, HPAD), lambda i: (i, 0))] * 2,
        out_shape=[jax.ShapeDtypeStruct((t, HPAD), F32)] * 2,
        compiler_params=_params("parallel"),
        name="rope_tables",
    )(pos_f, inv_row)


def _memkv_kernel(mem_ref, g_ref, w_ref, kn_ref, k_ref, v_ref):
    mn = _rms(mem_ref[0], g_ref[0]).astype(BF16)
    kv = _dot(mn, w_ref[0])
    k = kv[:, :GROUP]
    ones_bd = jnp.where(_head_ones(GROUP, HDIM), 1.0, 0.0).astype(BF16)
    k_ref[0, 0] = (k * _head_rsqrt(k, ones_bd, HDIM) * kn_ref[0]).astype(BF16)
    v_ref[0, 0] = kv[:, GROUP:].astype(BF16)


def _mem_kv(mem, g_mem, wkv, kn_g):
    b, n, d = mem.shape
    nl = wkv.shape[0]
    return pl.pallas_call(
        _memkv_kernel,
        grid=(nl, b),
        in_specs=[pl.BlockSpec((1, n, d), lambda l, i: (i, 0, 0)),
                  pl.BlockSpec((1, 1, d), lambda l, i: (l, 0, 0)),
                  pl.BlockSpec((1, d, 2 * GROUP), lambda l, i: (l, 0, 0)),
                  pl.BlockSpec((1, 1, GROUP), lambda l, i: (l, 0, 0))],
        out_specs=[pl.BlockSpec((1, 1, n, GROUP), lambda l, i: (l, i, 0, 0))] * 2,
        out_shape=[jax.ShapeDtypeStruct((nl, b, n, GROUP), BF16)] * 2,
        compiler_params=_params("parallel", "parallel"),
        name="mem_kv",
    )(mem, g_mem, wkv, kn_g)


def _inproj_kernel(x_ref, xp_ref, xn_ref, g_ref, w_ref, ct_ref, sg_ref, qag_ref, kvag_ref, wq_ref, wk_ref,
                   gq_ref, gqs_ref, gk_ref, gks_ref, aw_ref, ab_ref, lng_ref, lnb_ref, cw_ref, cb_ref, gba_ref, gbc_ref,
                   ub_ref, q_ref, k_ref, v_ref, ya_ref, yc_ref, ha_scr, hc_scr, gate_scr,
                   *, widths, tiles_per_seq, conv_rows):
    wa, wb, wc, wd = widths
    i = pl.program_id(0) % tiles_per_seq
    norm = lambda x: _rms(x, g_ref[...]).astype(BF16)
    n = norm(x_ref[...])
    _mla_qkv(_dot(n, w_ref[0, :, wa + wb + wc:wa + wb + wc + wd]), ct_ref, sg_ref, qag_ref, kvag_ref, wq_ref, wk_ref,
             gq_ref, gqs_ref, gk_ref, gks_ref, q_ref, k_ref, v_ref)
    n_ext = jnp.concatenate([norm(xp_ref[...]), n, norm(xn_ref[...])], axis=0)
    conv_fill, conv_tile = _conv_parts(aw_ref, ab_ref, lng_ref, lnb_ref, cw_ref, cb_ref, gba_ref, gbc_ref,
                                       ya_ref, yc_ref, ha_scr, hc_scr, gate_scr, conv_rows)
    conv_fill(_dot(n_ext, w_ref[0, :, 0:wa]), _dot(n_ext, w_ref[0, :, wa + wb:wa + wb + wc]),
              jnp.where(i > 0, 1.0, 0.0), jnp.where(i < tiles_per_seq - 1, 1.0, 0.0))
    tiles = list(range(0, x_ref.shape[0], conv_rows))
    groups = wb // GROUP
    for c in range(groups):
        blk = _dot(n, w_ref[0, :, wa + c * GROUP:wa + (c + 1) * GROUP])
        ub_ref[:, c * GROUP:(c + 1) * GROUP] = blk
        bits = pltpu.bitcast(blk[-1:, :], jnp.uint32)
        zero = pltpu.bitcast(lax.shift_right_logical(bits, jnp.uint32(32)), F32)
        for r0 in tiles[c * len(tiles) // groups:(c + 1) * len(tiles) // groups]:
            conv_tile(r0, zero)


def _in_proj(x2, g, w, ct, sg, mla, conv, widths, layer, tm, seq):
    t, d = x2.shape
    nw = HEADS * HPAD
    hb = tm // HALO
    last = t // HALO - 1
    row = lambda n: pl.BlockSpec((tm, n), lambda i: (i, 0))
    return pl.pallas_call(
        functools.partial(_inproj_kernel, widths=widths, tiles_per_seq=seq // tm, conv_rows=CONV_ROWS),
        grid=(t // tm,),
        in_specs=[row(d),
                  pl.BlockSpec((HALO, d), lambda i: (jnp.maximum(i * hb - 1, 0), 0)),
                  pl.BlockSpec((HALO, d), lambda i: (jnp.minimum((i + 1) * hb, last), 0)),
                  _full((1, d)),
                  pl.BlockSpec((1,) + w.shape[1:], lambda i: (layer, 0, 0), pipeline_mode=pl.Buffered(1)),
                  row(HPAD), row(HPAD)] + [_full(a.shape) for a in mla + conv],
        out_specs=[row(widths[1]), row(nw), row(nw), pl.BlockSpec((HEADS, tm, HPAD), lambda i: (0, i, 0)),
                   row(GROUP), row(GROUP)],
        out_shape=[jax.ShapeDtypeStruct((t, widths[1]), F32),
                   jax.ShapeDtypeStruct((t, nw), BF16), jax.ShapeDtypeStruct((t, nw), BF16),
                   jax.ShapeDtypeStruct((HEADS, t, HPAD), BF16),
                   jax.ShapeDtypeStruct((t, GROUP), BF16), jax.ShapeDtypeStruct((t, GROUP), BF16)],
        scratch_shapes=[pltpu.VMEM((tm + 2 * HALO, GROUP), F32)] * 2 + [pltpu.VMEM((tm, GROUP), F32)],
        compiler_params=_params("parallel"),
        name="in_proj",
    )(x2, x2, x2, g, w, ct, sg, *mla, *conv)


def _conv_parts(aw_ref, ab_ref, lng_ref, lnb_ref, cw_ref, cb_ref, gba_ref, gbc_ref,
                ya_ref, yc_ref, ha_scr, hc_scr, gate_scr, sub):
    tc = gate_scr.shape[0]
    pad_a = (CONF_W - 1) // 2
    pad_c = (SC_W - 1) // 2

    def fill(ua, uc, has_prev, has_next):
        ha = ua[:, :GROUP] * _sigmoid(ua[:, GROUP:])
        hc = uc[:, GROUP:2 * GROUP] * uc[:, 2 * GROUP:]
        for scr, h in ((ha_scr, ha), (hc_scr, hc)):
            scr[0:HALO, :] = h[0:HALO] * has_prev
            scr[HALO:HALO + tc, :] = h[HALO:HALO + tc]
            scr[HALO + tc:, :] = h[HALO + tc:] * has_next
        gate_scr[...] = uc[HALO:HALO + tc, :GROUP]

    def tile(r0, zero):
        acc = jnp.broadcast_to(ab_ref[...] + zero, (sub, GROUP))
        for r in range(SUBLANES):
            part = None
            for j in range(CONF_W):
                start = HALO - pad_a + j
                if start % SUBLANES != r:
                    continue
                lo = r0 + start - r
                term = aw_ref[j:j + 1, :] * ha_scr[lo:lo + sub + SUBLANES, :]
                part = term if part is None else part + term
            if part is not None:
                acc = acc + part[r:r + sub, :]
        mu = jnp.mean(acc, axis=-1, keepdims=True)
        xc = acc - mu
        y = xc * lax.rsqrt(jnp.mean(xc * xc, axis=-1, keepdims=True) + EPS) * lng_ref[...] + lnb_ref[...]
        y = y * _sigmoid(y)
        ya_ref[r0:r0 + sub, :] = _rms(y, gba_ref[...]).astype(BF16)

        acc = jnp.broadcast_to(cb_ref[...], (sub, GROUP))
        for j in range(SC_W):
            acc = acc + cw_ref[j:j + 1, :] * hc_scr[r0 + HALO - pad_c + j:r0 + HALO - pad_c + j + sub, :]
        y = gate_scr[r0:r0 + sub, :] * acc
        yc_ref[r0:r0 + sub, :] = _rms(y, gbc_ref[...]).astype(BF16)

    return fill, tile


def _hgrn_parts(gam_ref, qf_ref, zf_ref, vf_ref, qb_ref, zb_ref, vb_ref, of_ref, ob_ref,
                st_f, st_b, b_f, b_b, c_f, c_b, lam_f, lam_b, qe_f, qe_b, kt_f, kt_b,
                sts_f, sts_b, ones_scr, layer):
    tb = qf_ref.shape[1]
    nst = tb // STEP
    half = STEP // 2
    cs = 256

    ones_scr[...] = jnp.where(_head_ones(GROUP, HDIM), 1.0, 0.0).astype(BF16)

    ti = lax.broadcasted_iota(jnp.int32, (cs, cs), 0)
    si = lax.broadcasted_iota(jnp.int32, (cs, cs), 1)
    same = (ti // STEP) == (si // STEP)
    m_all = jnp.where(same, 1.0, 0.0).astype(BF16)

    def prep(d, q_ref, z_ref, b_scr, c_scr, lam_scr, qe_scr, kt_scr):
        g = gam_ref[d]
        e = jnp.exp(g - jnp.max(g, axis=0, keepdims=True))
        tot = jnp.sum(e, axis=0, keepdims=True)
        if layer == 0:
            lb = jnp.zeros_like(tot)
        else:
            lb = jnp.sum(e[1:layer + 1], axis=0, keepdims=True) / tot
        tri = (si <= ti) if d == 0 else (si >= ti)
        m_tri = jnp.where(same & tri, 1.0, 0.0).astype(BF16)
        for r0 in range(0, tb, cs):
            z = z_ref[0, r0:r0 + cs, :]
            a = jnp.exp(-jnp.abs(z))
            big = 1.0 / (1.0 + a)
            pos = z >= 0.0
            f = lb + (1.0 - lb) * jnp.where(pos, big, a * big)
            kk = (1.0 - lb) * jnp.where(pos, a * big, big)
            lf = jnp.log2(f)
            bloc = _split_dot(m_tri, lf)
            btot = _split_dot(m_all, lf)
            b_scr[r0:r0 + cs, :] = bloc
            c_scr[r0:r0 + cs, :] = bloc - jnp.log2(kk)
            lam_scr[r0:r0 + cs, :] = jnp.exp2(btot)
            qe_scr[r0:r0 + cs, :] = (q_ref[0, r0:r0 + cs, :] * jnp.exp2(bloc)).astype(BF16)
            kt_scr[r0:r0 + cs, :] = (kk * jnp.exp2(btot - bloc)).astype(BF16)

    prep(0, qf_ref, zf_ref, b_f, c_f, lam_f, qe_f, kt_f)
    prep(1, qb_ref, zb_ref, b_b, c_b, lam_b, qe_b, kt_b)

    trow = lax.broadcasted_iota(jnp.int32, (half, GROUP), 0)
    head_of_lane = lax.broadcasted_iota(jnp.int32, (STEP, GROUP), 1) // HDIM

    def liveness(s, rev):
        if rev:
            return (1, 0) if s < half else (2, 1)
        return (1, 2) if s < half else (0, 1)

    def per_head_rows(x):
        zero = jnp.zeros_like(x)
        return jnp.concatenate([jnp.where(head_of_lane == h, x, zero) for h in range(HEADS)], axis=0)

    def advance(k, r, state, v_ref, kt_scr, lam_scr, sts_scr):
        vb = v_ref[0, pl.ds(r, STEP), :].astype(BF16)
        vstack = jnp.concatenate([vb[:, h * HDIM:(h + 1) * HDIM] for h in range(HEADS)], axis=0)
        upd = _dot_tn(vstack, per_head_rows(kt_scr[pl.ds(r, STEP), :]))
        sts_scr[pl.ds(k * HDIM, HDIM), :] = state.astype(BF16)
        return lam_scr[pl.ds(r, 1), :] * state + upd

    def output(k, r, rev, q_ref, v_ref, o_ref, b_scr, c_scr, qe_scr, sts_scr):
        res = _dot_nt(per_head_rows(qe_scr[pl.ds(r, STEP), :]), sts_scr[pl.ds(k * HDIM, HDIM), :])
        o = jnp.concatenate([res[h * STEP:(h + 1) * STEP, :] for h in range(HEADS)], axis=1)
        o = [o[:half], o[half:]]
        q = (q_ref[0, pl.ds(r, half), :], q_ref[0, pl.ds(r + half, half), :])
        b = (b_scr[pl.ds(r, half), :], b_scr[pl.ds(r + half, half), :])
        pieces, where_to = [], []
        for s in range(STEP):
            crow = jnp.broadcast_to(c_scr[pl.ds(r + s, 1), :], (half, GROUP))
            for hx, kind in enumerate(liveness(s, rev)):
                if kind == 0:
                    continue
                dec = q[hx] * jnp.exp2(b[hx] - crow)
                if kind == 1:
                    sl = s - hx * half
                    dec = jnp.where((trow <= sl) if rev else (trow >= sl), dec, 0.0)
                pieces.append(dec)
                where_to.append((s, hx))
        att = _dot(jnp.concatenate(pieces, axis=0).astype(BF16), ones_scr[...])
        vrows = [jnp.broadcast_to(v_ref[0, pl.ds(r + s, 1), :], (half, GROUP)) for s in range(STEP)]
        for n, (s, hx) in enumerate(where_to):
            o[hx] = o[hx] + att[n * half:(n + 1) * half, :] * vrows[s]
        o_ref[0, pl.ds(r, half), :] = o[0]
        o_ref[0, pl.ds(r + half, half), :] = o[1]

    states = [st_f[...], st_b[...]]

    def state_step(k):
        states[0] = advance(k, k * STEP, states[0], vf_ref, kt_f, lam_f, sts_f)
        states[1] = advance(k, (nst - 1 - k) * STEP, states[1], vb_ref, kt_b, lam_b, sts_b)
        if k == nst - 1:
            st_f[...] = states[0]
            st_b[...] = states[1]

    def output_step(k):
        output(k, k * STEP, False, qf_ref, vf_ref, of_ref, b_f, c_f, qe_f, sts_f)
        output(k, (nst - 1 - k) * STEP, True, qb_ref, vb_ref, ob_ref, b_b, c_b, qe_b, sts_b)

    return nst, state_step, output_step


def _hgrn_scratch(tb):
    vm = lambda dt: pltpu.VMEM((tb, GROUP), dt)
    return ([pltpu.VMEM((HDIM, GROUP), F32)] * 2 + [vm(F32)] * 6 + [vm(BF16)] * 4
            + [pltpu.VMEM((tb // STEP * HDIM, GROUP), BF16)] * 2
            + [pltpu.VMEM((GROUP, GROUP), BF16)])


def _mla_qkv(ud, ct_ref, sg_ref, qag_ref, kvag_ref, wq_ref, wk_ref,
             gq_ref, gqs_ref, gk_ref, gks_ref, q_ref, k_ref, v_ref):
    nw = HEADS * HPAD
    nq = _rms(ud[:, :GROUP], qag_ref[...]).astype(BF16)
    qq = _dot(nq, wq_ref[...])
    nkv = _rms(ud[:, GROUP:GROUP + HPAD], kvag_ref[...]).astype(BF16)
    kr = ud[:, GROUP + HPAD:]
    kr_hi = kr.astype(BF16)
    kr_lo = (kr - kr_hi.astype(F32)).astype(BF16)
    kk = _dot(jnp.concatenate([nkv, kr_hi, kr_lo], axis=-1), wk_ref[...])
    ct = ct_ref[...]
    sg = sg_ref[...]
    scale = QK ** -0.5 * LOG2E
    for h in range(HEADS):
        sl = slice(h * HPAD, (h + 1) * HPAD)
        sw = slice(nw + h * HPAD, nw + (h + 1) * HPAD)
        x = qq[:, sl]
        r = lax.rsqrt(jnp.sum(x * x, axis=-1, keepdims=True) * (1.0 / QK) + EPS) * scale
        q_ref[:, sl] = (r * (x * (gq_ref[...] * ct) + qq[:, sw] * (gqs_ref[...] * sg))).astype(BF16)
        x = kk[:, sl]
        r = lax.rsqrt(jnp.sum(x * x, axis=-1, keepdims=True) * (1.0 / QK) + EPS)
        k_ref[:, sl] = (r * (x * (gk_ref[...] * ct) + kk[:, sw] * (gks_ref[...] * sg))).astype(BF16)
    first = lax.broadcasted_iota(jnp.int32, (ud.shape[0], HPAD), 1) < HDIM
    for pair in range(HEADS // 2):
        vv = kk[:, 2 * nw + pair * HPAD:2 * nw + (pair + 1) * HPAD]
        v_ref[2 * pair] = jnp.where(first, vv, 1.0).astype(BF16)
        v_ref[2 * pair + 1] = jnp.where(first, 1.0, vv).astype(BF16)


def _attn_parts(q_ref, k_ref, v_ref, gb_ref, o_ref, sb_scr, rows, ck):
    tq = q_ref.shape[1]
    n_keys = k_ref.shape[1]
    first = lax.broadcasted_iota(jnp.int32, (rows, HPAD), 1) < HDIM
    items = [(r0, h) for r0 in range(0, tq, rows) for h in range(HEADS)]
    done = []

    def scores(n):
        r0, h = items[n]
        sl = slice(h * HPAD, (h + 1) * HPAD)
        q = q_ref[0, r0:r0 + rows, sl]
        top = None
        for c in range(0, n_keys, ck):
            sb = _dot_nt(q, k_ref[0, c:c + ck, sl]).astype(BF16)
            sb_scr[n % 2, :, c:c + ck] = sb
            for j in range(0, ck, HPAD):
                top = sb[:, j:j + HPAD] if top is None else jnp.maximum(top, sb[:, j:j + HPAD])
        return jnp.max(top, axis=-1, keepdims=True)

    def finish(n, m):
        r0, h = items[n]
        w = None
        for c in range(0, n_keys, ck):
            p = jnp.exp2(sb_scr[n % 2, :, c:c + ck] - m)
            part = _dot(p, v_ref[h, 0, c:c + ck, :])
            w = part if w is None else w + part
        w = w / pltpu.roll(w, HDIM, axis=1)
        done.append(w)
        if h == HEADS - 1:
            pairs = [jnp.where(first, done[2 * j], done[2 * j + 1]) for j in range(HEADS // 2)]
            del done[:]
            o_ref[0, r0:r0 + rows, :] = _rms(jnp.concatenate(pairs, axis=1), gb_ref[...]).astype(BF16)

    return items, scores, finish


N_HGRN_SCRATCH = 15


def _mixers_kernel(*refs, layer, rows, key_chunk):
    (gam_ref, qf_ref, zf_ref, vf_ref, qb_ref, zb_ref, vb_ref, q_ref, k_ref, v_ref, gbd_ref,
     of_ref, ob_ref, yd_ref) = refs[:14]
    hgrn_scr = refs[14:14 + N_HGRN_SCRATCH]
    sb_scr, = refs[14 + N_HGRN_SCRATCH:]

    @pl.when(pl.program_id(1) == 0)
    def _():
        for st in hgrn_scr[:2]:
            st[...] = jnp.zeros_like(st)

    nst, state_step, output_step = _hgrn_parts(gam_ref, qf_ref, zf_ref, vf_ref, qb_ref, zb_ref, vb_ref,
                                               of_ref, ob_ref, *hgrn_scr, layer)
    items, scores, finish = _attn_parts(q_ref, k_ref, v_ref, gbd_ref, yd_ref, sb_scr, rows, key_chunk)

    def share(seq, n, first, count):
        n -= first
        return seq[n * len(seq) // count:(n + 1) * len(seq) // count] if 0 <= n < count else []

    state_items = len(items) // 2
    first_out = len(items) // 4
    m_next = scores(0)
    for n in range(len(items)):
        m = m_next
        if n + 1 < len(items):
            m_next = scores(n + 1)
        finish(n, m)
        for k in share(range(nst), n, 0, state_items):
            state_step(k)
        for k in share(range(nst), n, first_out, len(items) - first_out):
            output_step(k)


def _mixers(ub, q, k, v, gamma, gbd, layer, tile):
    b, s, _ = ub.shape
    nb = s // tile
    col = lambda j, rev: pl.BlockSpec(
        (1, tile, GROUP), (lambda bi, i: (bi, nb - 1 - i, j)) if rev else (lambda bi, i: (bi, i, j)))
    main = lambda w: pl.BlockSpec((1, tile, w), lambda bi, i: (bi, i, 0))
    nw = q.shape[2]
    return pl.pallas_call(
        functools.partial(_mixers_kernel, layer=layer, rows=SUB_ROWS, key_chunk=KEY_CHUNK),
        grid=(b, nb),
        in_specs=[_full(gamma.shape),
                  col(0, False), col(1, False), col(3, False), col(0, True), col(2, True), col(3, True),
                  main(nw), pl.BlockSpec((1, s, nw), lambda bi, i: (bi, 0, 0)),
                  pl.BlockSpec((HEADS, 1, s, HPAD), lambda bi, i: (0, bi, 0, 0)),
                  _full(gbd.shape)],
        out_specs=[main(GROUP), pl.BlockSpec((1, tile, GROUP), lambda bi, i: (bi, nb - 1 - i, 0)), main(GROUP)],
        out_shape=[jax.ShapeDtypeStruct((b, s, GROUP), F32)] * 2 + [jax.ShapeDtypeStruct((b, s, GROUP), BF16)],
        scratch_shapes=_hgrn_scratch(tile) + [pltpu.VMEM((2, SUB_ROWS, s), BF16)],
        compiler_params=_params("parallel", "arbitrary"),
        name="mixers",
    )(gamma, ub, ub, ub, ub, ub, ub, q, k, v, gbd)


def _channel_kernel(x_ref, ya_ref, of_ref, ob_ref, g_ref, yc_ref, yd_ref, on_ref, gbb_ref, wo_ref,
                    gxq_ref, wq_ref, qn_ref, kx_ref, vx_ref, xwo_ref, gf_ref, w13_ref, w2_ref, o_ref, *, rows):
    tm = x_ref.shape[1]
    dff = w2_ref.shape[1]
    ones_bd = jnp.where(_head_ones(GROUP, HDIM), 1.0, 0.0).astype(BF16)
    lane = lax.broadcasted_iota(jnp.int32, (rows, GROUP), 1) // HDIM

    def mix(r0):
        sl = slice(r0, r0 + rows)
        o = of_ref[0, sl, :] + ob_ref[0, sl, :]
        g = g_ref[0, sl, :]
        yb = o * _head_rsqrt(o, ones_bd, HDIM) * on_ref[...] * (g * _sigmoid(g))
        yb = _rms(yb, gbb_ref[...]).astype(BF16)
        x1 = x_ref[0, sl, :]
        for j, y in enumerate((ya_ref[0, sl, :], yb, yc_ref[0, sl, :], yd_ref[0, sl, :])):
            x1 = x1 + _dot(y, wo_ref[0, j * GROUP:(j + 1) * GROUP, :])
        q = _dot(_rms(x1, gxq_ref[...]).astype(BF16), wq_ref[0])
        qn = q * _head_rsqrt(q, ones_bd, HDIM) * (qn_ref[...] * HDIM ** -0.5)
        acc = jnp.zeros((rows, GROUP), F32)
        for h in range(HEADS):
            s = _dot_nt(jnp.where(lane == h, qn, 0.0).astype(BF16), kx_ref[0, 0])
            p = jnp.exp(s - jnp.max(s, axis=-1, keepdims=True))
            l = jnp.sum(p, axis=-1, keepdims=True)
            pv = _dot(p.astype(BF16), vx_ref[0, 0])
            acc = jnp.where(lane == h, pv / l, acc)
        return x1 + _dot(acc.astype(BF16), xwo_ref[0])

    def ffn(x):
        a = _dot(_rms(x, gf_ref[...]).astype(BF16), w13_ref[0])
        a1 = a[:, :dff]
        return x + _dot((a1 * _sigmoid(a1) * a[:, dff:]).astype(BF16), w2_ref[0])

    starts = range(0, tm, rows)
    mixed = [mix(r0) for r0 in starts]
    for r0, x in zip(starts, mixed):
        o_ref[0, r0:r0 + rows, :] = ffn(x)


def _channel(x, ya, of, ob, ub, yc, yd, on, gbb, wo, gxq, wq, qn, kx, vx, xwo, gf, w13, w2, layer, tm, rows):
    b, s, d = x.shape
    n = kx.shape[2]
    row = lambda w, j=0: pl.BlockSpec((1, tm, w), lambda bi, i: (bi, i, j))
    of_layer = lambda a: pl.BlockSpec((1,) + a.shape[1:], lambda bi, i: (layer,) + (0,) * (a.ndim - 1),
                                      pipeline_mode=pl.Buffered(1))
    return pl.pallas_call(
        functools.partial(_channel_kernel, rows=rows),
        grid=(b, s // tm),
        in_specs=[row(d), row(GROUP), row(GROUP), row(GROUP), row(GROUP, 4), row(GROUP), row(GROUP),
                  _full(on.shape), _full(gbb.shape), of_layer(wo), _full(gxq.shape), of_layer(wq),
                  _full(qn.shape),
                  pl.BlockSpec((1, 1, n, GROUP), lambda bi, i: (layer, bi, 0, 0)),
                  pl.BlockSpec((1, 1, n, GROUP), lambda bi, i: (layer, bi, 0, 0)),
                  of_layer(xwo), _full(gf.shape), of_layer(w13), of_layer(w2)],
        out_specs=row(d),
        out_shape=jax.ShapeDtypeStruct((b, s, d), F32),
        compiler_params=_params("parallel", "parallel"),
        name="channel",
    )(x, ya, of, ob, ub, yc, yd, on, gbb, wo, gxq, wq, qn, kx, vx, xwo, gf, w13, w2)


def _pad_heads(w, width):
    lead = w.shape[:-1]
    w = w.reshape(lead + (HEADS, width))
    w = jnp.pad(w, [(0, 0)] * len(lead) + [(0, 0), (0, HPAD - width)])
    return w.reshape(lead + (HEADS * HPAD,))


def _swap_rope(w):
    half = ROPE // 2
    z = jnp.zeros_like(w[..., :HDIM])
    return jnp.concatenate([z, w[..., HDIM + half:], w[..., HDIM:HDIM + half]], axis=-1)


def _mla_weights(wuq, wukv, qn_g, kn_g):
    dq = wuq.shape[0]
    dkv = wukv.shape[0]
    wq3 = wuq.reshape(dq, HEADS, QK)
    wq = jnp.concatenate([_pad_heads(wuq, QK), _pad_heads(_swap_rope(wq3).reshape(dq, -1), QK)], axis=1)
    kv3 = wukv.reshape(dkv, HEADS, 2 * HDIM)
    k_nope = jnp.pad(kv3[..., :HDIM], [(0, 0), (0, 0), (0, HPAD - HDIM)]).reshape(dkv, -1)
    v_cols = kv3[..., HDIM:].reshape(dkv, -1)
    top = jnp.concatenate([k_nope, jnp.zeros_like(k_nope), v_cols], axis=1)
    eye = jnp.eye(ROPE, dtype=F32)
    place = jnp.pad(eye, [(0, 0), (HDIM, HPAD - QK)])
    place_sw = jnp.pad(jnp.roll(eye, ROPE // 2, axis=1), [(0, 0), (HDIM, HPAD - QK)])
    rope_rows = jnp.concatenate([jnp.tile(place, (1, HEADS)), jnp.tile(place_sw, (1, HEADS)),
                                 jnp.zeros((ROPE, v_cols.shape[1]), F32)], axis=1)
    rope_rows = jnp.pad(rope_rows, [(0, HPAD - ROPE), (0, 0)])
    wk = jnp.concatenate([top, rope_rows, rope_rows], axis=0)
    pad1 = lambda g: jnp.pad(g, (0, HPAD - QK))[None, :]
    return (wq.astype(BF16), wk.astype(BF16),
            pad1(qn_g), pad1(_swap_rope(qn_g)), pad1(kn_g), pad1(_swap_rope(kn_g)))


def kernel(x, mem, positions, g_mix, w_in, a_dw_w, a_dw_b, a_ln_g, a_ln_b, h_gamma, h_onorm_g, c_dw_w, c_dw_b, m_qa_g, m_wuq, m_kva_g, m_wukv, m_qn_g, m_kn_g, g_branch, w_out, g_xq, g_mem, x_wq, x_wkv, x_qn_g, x_kn_g, x_wo, g_ffn, f_w13, f_w2):
    b, s, d = x.shape
    t = b * s
    depth = w_in.shape[0]
    row = lambda v: v[None, :]
    tile4 = lambda v: jnp.tile(v, HEADS)[None, :]

    inv = ROPE_BASE ** (-jnp.arange(0, ROPE, 2, dtype=F32) / ROPE)
    inv_row = jnp.concatenate([jnp.zeros((HDIM,), F32), inv, inv, jnp.zeros((HPAD - QK,), F32)])[None, :]
    ct, sg = _rope_tables(positions.astype(F32).reshape(t, 1), inv_row, TOKEN_TILE)

    kv4 = x_wkv.reshape(depth, d, HEADS, 2, HDIM).transpose(0, 1, 3, 2, 4).reshape(depth, d, 2 * GROUP)
    kx, vx = _mem_kv(mem, g_mem[:, None, :], kv4.astype(BF16), jnp.tile(x_kn_g, (1, HEADS))[:, None, :])

    wo_all, xwq_all, xwo_all = w_out.astype(BF16), x_wq.astype(BF16), x_wo.astype(BF16)
    w13_all, w2_all = f_w13.astype(BF16), f_w2.astype(BF16)
    widths = (2 * GROUP, 5 * GROUP, 3 * GROUP, 2 * GROUP)
    w_in_all = jnp.pad(w_in, [(0, 0), (0, 0), (0, sum(widths) - w_in.shape[2])]).astype(BF16)
    for l in range(depth):
        mla = (row(m_qa_g[l]), row(m_kva_g[l])) + _mla_weights(m_wuq[l], m_wukv[l], m_qn_g[l], m_kn_g[l])
        gb = g_branch[l]
        conv = (a_dw_w[l], row(a_dw_b[l]), row(a_ln_g[l]), row(a_ln_b[l]), c_dw_w[l], row(c_dw_b[l]),
                row(gb[:GROUP]), row(gb[2 * GROUP:3 * GROUP]))
        ub, qr, kr, vr, ya, yc = _in_proj(x.reshape(t, d), row(g_mix[l]), w_in_all, ct, sg, mla, conv,
                                          widths, l, TOKEN_TILE, s)
        ub = ub.reshape(b, s, -1)
        of, ob, yd = _mixers(ub, qr.reshape(b, s, -1), kr.reshape(b, s, -1), vr.reshape(HEADS, b, s, HPAD),
                             h_gamma, row(gb[3 * GROUP:]), l, TOKEN_TILE)
        ya = ya.reshape(b, s, -1)
        yc = yc.reshape(b, s, -1)
        x = _channel(x, ya, of, ob, ub, yc, yd, row(h_onorm_g[l]), row(gb[GROUP:2 * GROUP]), wo_all,
                     row(g_xq[l]), xwq_all, tile4(x_qn_g[l]), kx, vx, xwo_all, row(g_ffn[l]), w13_all, w2_all,
                     l, TOKEN_TILE, SUB_ROWS)
    return x
```

```python
import functools

import jax
import jax.numpy as jnp
from jax import lax
from jax.experimental import pallas as pl
from jax.experimental.pallas import tpu as pltpu

F32 = jnp.float32
BF16 = jnp.bfloat16
EPS = 1e-6

GROUP = 256
HEADS = 4
HDIM = 64
ROPE = 32
QK = HDIM + ROPE
HPAD = 128
CONF_W = 31
SC_W = 3
HALO = 16
STEP = 16
SUBLANES = 8
ROPE_BASE = 10000.0
LOG2E = 1.4426950408889634
VMEM_LIMIT = 60 * 1024 * 1024

TOKEN_TILE = 512
SUB_ROWS = 256
KEY_CHUNK = 512
CONV_ROWS = 64


def _rms(x, g):
    return x * lax.rsqrt(jnp.mean(x * x, axis=-1, keepdims=True) + EPS) * g


def _sigmoid(x):
    return 1.0 / (1.0 + jnp.exp(-x))


def _dot(a, b):
    return jnp.dot(a, b, preferred_element_type=F32)


def _dot_nt(a, b):
    return lax.dot_general(a, b, (((1,), (1,)), ((), ())), preferred_element_type=F32)


def _dot_tn(a, b):
    return lax.dot_general(a, b, (((0,), (0,)), ((), ())), preferred_element_type=F32)


def _split_dot(m, x):
    hi = x.astype(BF16)
    lo = (x - hi.astype(F32)).astype(BF16)
    return _dot(m, hi) + _dot(m, lo)


def _head_ones(n, width):
    r = lax.broadcasted_iota(jnp.int32, (n, n), 0) // width
    c = lax.broadcasted_iota(jnp.int32, (n, n), 1) // width
    return r == c


def _head_rsqrt(x, ones_bd, width):
    return lax.rsqrt(_split_dot_rhs(x * x, ones_bd) * (1.0 / width) + EPS)


def _split_dot_rhs(x, m):
    hi = x.astype(BF16)
    lo = (x - hi.astype(F32)).astype(BF16)
    return _dot(hi, m) + _dot(lo, m)


def _params(*sem):
    return pltpu.CompilerParams(dimension_semantics=sem, vmem_limit_bytes=VMEM_LIMIT)


def _full(shape):
    nd = len(shape)
    return pl.BlockSpec(shape, lambda *_: (0,) * nd)


def _rope_kernel(pos_ref, inv_ref, ct_ref, sg_ref):
    ang = pos_ref[...] * inv_ref[...]
    lane = lax.broadcasted_iota(jnp.int32, ang.shape, 1)
    c = jnp.cos(ang)
    s = jnp.sin(ang)
    half = ROPE // 2
    ct_ref[...] = jnp.where(lane < HDIM, 1.0, jnp.where(lane < QK, c, 0.0))
    sg_ref[...] = jnp.where(lane < HDIM, 0.0,
                            jnp.where(lane < HDIM + half, -s, jnp.where(lane < QK, s, 0.0)))


def _rope_tables(pos_f, inv_row, tm):
    t = pos_f.shape[0]
    return pl.pallas_call(
        _rope_kernel,
        grid=(t // tm,),
        in_specs=[pl.BlockSpec((tm, 1), lambda i: (i, 0)), _full((1, HPAD))],
        out_specs=[pl.BlockSpec((tm, HPAD), lambda i: (i, 0))] * 2,
        out_shape=[jax.ShapeDtypeStruct((t, HPAD), F32)] * 2,
        compiler_params=_params("parallel"),
        name="rope_tables",
    )(pos_f, inv_row)


def _memkv_kernel(mem_ref, g_ref, w_ref, kn_ref, k_ref, v_ref):
    mn = _rms(mem_ref[0], g_ref[0]).astype(BF16)
    kv = _dot(mn, w_ref[0])
    k = kv[:, :GROUP]
    ones_bd = jnp.where(_head_ones(GROUP, HDIM), 1.0, 0.0).astype(BF16)
    k_ref[0, 0] = (k * _head_rsqrt(k, ones_bd, HDIM) * kn_ref[0]).astype(BF16)
    v_ref[0, 0] = kv[:, GROUP:].astype(BF16)


def _mem_kv(mem, g_mem, wkv, kn_g):
    b, n, d = mem.shape
    nl = wkv.shape[0]
    return pl.pallas_call(
        _memkv_kernel,
        grid=(nl, b),
        in_specs=[pl.BlockSpec((1, n, d), lambda l, i: (i, 0, 0)),
                  pl.BlockSpec((1, 1, d), lambda l, i: (l, 0, 0)),
                  pl.BlockSpec((1, d, 2 * GROUP), lambda l, i: (l, 0, 0)),
                  pl.BlockSpec((1, 1, GROUP), lambda l, i: (l, 0, 0))],
        out_specs=[pl.BlockSpec((1, 1, n, GROUP), lambda l, i: (l, i, 0, 0))] * 2,
        out_shape=[jax.ShapeDtypeStruct((nl, b, n, GROUP), BF16)] * 2,
        compiler_params=_params("parallel", "parallel"),
        name="mem_kv",
    )(mem, g_mem, wkv, kn_g)


def _inproj_kernel(x_ref, xp_ref, xn_ref, g_ref, w_ref, ct_ref, sg_ref, qag_ref, kvag_ref, wq_ref, wk_ref,
                   gq_ref, gqs_ref, gk_ref, gks_ref, aw_ref, ab_ref, lng_ref, lnb_ref, cw_ref, cb_ref, gba_ref, gbc_ref,
                   ub_ref, q_ref, k_ref, v_ref, ya_ref, yc_ref, ha_scr, hc_scr, gate_scr,
                   *, widths, tiles_per_seq, conv_rows):
    wa, wb, wc, wd = widths
    i = pl.program_id(0) % tiles_per_seq
    norm = lambda x: _rms(x, g_ref[...]).astype(BF16)
    n = norm(x_ref[...])
    _mla_qkv(_dot(n, w_ref[0, :, wa + wb + wc:wa + wb + wc + wd]), ct_ref, sg_ref, qag_ref, kvag_ref, wq_ref, wk_ref,
             gq_ref, gqs_ref, gk_ref, gks_ref, q_ref, k_ref, v_ref)
    n_ext = jnp.concatenate([norm(xp_ref[...]), n, norm(xn_ref[...])], axis=0)
    conv_fill, conv_tile = _conv_parts(aw_ref, ab_ref, lng_ref, lnb_ref, cw_ref, cb_ref, gba_ref, gbc_ref,
                                       ya_ref, yc_ref, ha_scr, hc_scr, gate_scr, conv_rows)
    conv_fill(_dot(n_ext, w_ref[0, :, 0:wa]), _dot(n_ext, w_ref[0, :, wa + wb:wa + wb + wc]),
              jnp.where(i > 0, 1.0, 0.0), jnp.where(i < tiles_per_seq - 1, 1.0, 0.0))
    tiles = list(range(0, x_ref.shape[0], conv_rows))
    groups = wb // GROUP
    for c in range(groups):
        blk = _dot(n, w_ref[0, :, wa + c * GROUP:wa + (c + 1) * GROUP])
        ub_ref[:, c * GROUP:(c + 1) * GROUP] = blk
        bits = pltpu.bitcast(blk[-1:, :], jnp.uint32)
        zero = pltpu.bitcast(lax.shift_right_logical(bits, jnp.uint32(32)), F32)
        for r0 in tiles[c * len(tiles) // groups:(c + 1) * len(tiles) // groups]:
            conv_tile(r0, zero)


def _in_proj(x2, g, w, ct, sg, mla, conv, widths, layer, tm, seq):
    t, d = x2.shape
    nw = HEADS * HPAD
    hb = tm // HALO
    last = t // HALO - 1
    row = lambda n: pl.BlockSpec((tm, n), lambda i: (i, 0))
    return pl.pallas_call(
        functools.partial(_inproj_kernel, widths=widths, tiles_per_seq=seq // tm, conv_rows=CONV_ROWS),
        grid=(t // tm,),
        in_specs=[row(d),
                  pl.BlockSpec((HALO, d), lambda i: (jnp.maximum(i * hb - 1, 0), 0)),
                  pl.BlockSpec((HALO, d), lambda i: (jnp.minimum((i + 1) * hb, last), 0)),
                  _full((1, d)),
                  pl.BlockSpec((1,) + w.shape[1:], lambda i: (layer, 0, 0), pipeline_mode=pl.Buffered(1)),
                  row(HPAD), row(HPAD)] + [_full(a.shape) for a in mla + conv],
        out_specs=[row(widths[1]), row(nw), row(nw), pl.BlockSpec((HEADS, tm, HPAD), lambda i: (0, i, 0)),
                   row(GROUP), row(GROUP)],
        out_shape=[jax.ShapeDtypeStruct((t, widths[1]), F32),
                   jax.ShapeDtypeStruct((t, nw), BF16), jax.ShapeDtypeStruct((t, nw), BF16),
                   jax.ShapeDtypeStruct((HEADS, t, HPAD), BF16),
                   jax.ShapeDtypeStruct((t, GROUP), BF16), jax.ShapeDtypeStruct((t, GROUP), BF16)],
        scratch_shapes=[pltpu.VMEM((tm + 2 * HALO, GROUP), F32)] * 2 + [pltpu.VMEM((tm, GROUP), F32)],
        compiler_params=_params("parallel"),
        name="in_proj",
    )(x2, x2, x2, g, w, ct, sg, *mla, *conv)


def _conv_parts(aw_ref, ab_ref, lng_ref, lnb_ref, cw_ref, cb_ref, gba_ref, gbc_ref,
                ya_ref, yc_ref, ha_scr, hc_scr, gate_scr, sub):
    tc = gate_scr.shape[0]
    pad_a = (CONF_W - 1) // 2
    pad_c = (SC_W - 1) // 2

    def fill(ua, uc, has_prev, has_next):
        ha = ua[:, :GROUP] * _sigmoid(ua[:, GROUP:])
        hc = uc[:, GROUP:2 * GROUP] * uc[:, 2 * GROUP:]
        for scr, h in ((ha_scr, ha), (hc_scr, hc)):
            scr[0:HALO, :] = h[0:HALO] * has_prev
            scr[HALO:HALO + tc, :] = h[HALO:HALO + tc]
            scr[HALO + tc:, :] = h[HALO + tc:] * has_next
        gate_scr[...] = uc[HALO:HALO + tc, :GROUP]

    def tile(r0, zero):
        acc = jnp.broadcast_to(ab_ref[...] + zero, (sub, GROUP))
        for r in range(SUBLANES):
            part = None
            for j in range(CONF_W):
                start = HALO - pad_a + j
                if start % SUBLANES != r:
                    continue
                lo = r0 + start - r
                term = aw_ref[j:j + 1, :] * ha_scr[lo:lo + sub + SUBLANES, :]
                part = term if part is None else part + term
            if part is not None:
                acc = acc + part[r:r + sub, :]
        mu = jnp.mean(acc, axis=-1, keepdims=True)
        xc = acc - mu
        y = xc * lax.rsqrt(jnp.mean(xc * xc, axis=-1, keepdims=True) + EPS) * lng_ref[...] + lnb_ref[...]
        y = y * _sigmoid(y)
        ya_ref[r0:r0 + sub, :] = _rms(y, gba_ref[...]).astype(BF16)

        acc = jnp.broadcast_to(cb_ref[...], (sub, GROUP))
        for j in range(SC_W):
            acc = acc + cw_ref[j:j + 1, :] * hc_scr[r0 + HALO - pad_c + j:r0 + HALO - pad_c + j + sub, :]
        y = gate_scr[r0:r0 + sub, :] * acc
        yc_ref[r0:r0 + sub, :] = _rms(y, gbc_ref[...]).astype(BF16)

    return fill, tile


def _hgrn_parts(gam_ref, qf_ref, zf_ref, vf_ref, qb_ref, zb_ref, vb_ref, of_ref, ob_ref,
                st_f, st_b, b_f, b_b, c_f, c_b, lam_f, lam_b, qe_f, qe_b, kt_f, kt_b,
                sts_f, sts_b, ones_scr, layer):
    tb = qf_ref.shape[1]
    nst = tb // STEP
    half = STEP // 2
    cs = 256

    ones_scr[...] = jnp.where(_head_ones(GROUP, HDIM), 1.0, 0.0).astype(BF16)

    ti = lax.broadcasted_iota(jnp.int32, (cs, cs), 0)
    si = lax.broadcasted_iota(jnp.int32, (cs, cs), 1)
    same = (ti // STEP) == (si // STEP)
    m_all = jnp.where(same, 1.0, 0.0).astype(BF16)

    def prep(d, q_ref, z_ref, b_scr, c_scr, lam_scr, qe_scr, kt_scr):
        g = gam_ref[d]
        e = jnp.exp(g - jnp.max(g, axis=0, keepdims=True))
        tot = jnp.sum(e, axis=0, keepdims=True)
        if layer == 0:
            lb = jnp.zeros_like(tot)
        else:
            lb = jnp.sum(e[1:layer + 1], axis=0, keepdims=True) / tot
        tri = (si <= ti) if d == 0 else (si >= ti)
        m_tri = jnp.where(same & tri, 1.0, 0.0).astype(BF16)
        for r0 in range(0, tb, cs):
            z = z_ref[0, r0:r0 + cs, :]
            a = jnp.exp(-jnp.abs(z))
            big = 1.0 / (1.0 + a)
            pos = z >= 0.0
            f = lb + (1.0 - lb) * jnp.where(pos, big, a * big)
            kk = (1.0 - lb) * jnp.where(pos, a * big, big)
            lf = jnp.log2(f)
            bloc = _split_dot(m_tri, lf)
            btot = _split_dot(m_all, lf)
            b_scr[r0:r0 + cs, :] = bloc
            c_scr[r0:r0 + cs, :] = bloc - jnp.log2(kk)
            lam_scr[r0:r0 + cs, :] = jnp.exp2(btot)
            qe_scr[r0:r0 + cs, :] = (q_ref[0, r0:r0 + cs, :] * jnp.exp2(bloc)).astype(BF16)
            kt_scr[r0:r0 + cs, :] = (kk * jnp.exp2(btot - bloc)).astype(BF16)

    prep(0, qf_ref, zf_ref, b_f, c_f, lam_f, qe_f, kt_f)
    prep(1, qb_ref, zb_ref, b_b, c_b, lam_b, qe_b, kt_b)

    trow = lax.broadcasted_iota(jnp.int32, (half, GROUP), 0)
    head_of_lane = lax.broadcasted_iota(jnp.int32, (STEP, GROUP), 1) // HDIM

    def liveness(s, rev):
        if rev:
            return (1, 0) if s < half else (2, 1)
        return (1, 2) if s < half else (0, 1)

    def per_head_rows(x):
        zero = jnp.zeros_like(x)
        return jnp.concatenate([jnp.where(head_of_lane == h, x, zero) for h in range(HEADS)], axis=0)

    def advance(k, r, state, v_ref, kt_scr, lam_scr, sts_scr):
        vb = v_ref[0, pl.ds(r, STEP), :].astype(BF16)
        vstack = jnp.concatenate([vb[:, h * HDIM:(h + 1) * HDIM] for h in range(HEADS)], axis=0)
        upd = _dot_tn(vstack, per_head_rows(kt_scr[pl.ds(r, STEP), :]))
        sts_scr[pl.ds(k * HDIM, HDIM), :] = state.astype(BF16)
        return lam_scr[pl.ds(r, 1), :] * state + upd

    def output(k, r, rev, q_ref, v_ref, o_ref, b_scr, c_scr, qe_scr, sts_scr):
        res = _dot_nt(per_head_rows(qe_scr[pl.ds(r, STEP), :]), sts_scr[pl.ds(k * HDIM, HDIM), :])
        o = jnp.concatenate([res[h * STEP:(h + 1) * STEP, :] for h in range(HEADS)], axis=1)
        o = [o[:half], o[half:]]
        q = (q_ref[0, pl.ds(r, half), :], q_ref[0, pl.ds(r + half, half), :])
        b = (b_scr[pl.ds(r, half), :], b_scr[pl.ds(r + half, half), :])
        pieces, where_to = [], []
        for s in range(STEP):
            crow = jnp.broadcast_to(c_scr[pl.ds(r + s, 1), :], (half, GROUP))
            for hx, kind in enumerate(liveness(s, rev)):
                if kind == 0:
                    continue
                dec = q[hx] * jnp.exp2(b[hx] - crow)
                if kind == 1:
                    sl = s - hx * half
                    dec = jnp.where((trow <= sl) if rev else (trow >= sl), dec, 0.0)
                pieces.append(dec)
                where_to.append((s, hx))
        att = _dot(jnp.concatenate(pieces, axis=0).astype(BF16), ones_scr[...])
        vrows = [jnp.broadcast_to(v_ref[0, pl.ds(r + s, 1), :], (half, GROUP)) for s in range(STEP)]
        for n, (s, hx) in enumerate(where_to):
            o[hx] = o[hx] + att[n * half:(n + 1) * half, :] * vrows[s]
        o_ref[0, pl.ds(r, half), :] = o[0]
        o_ref[0, pl.ds(r + half, half), :] = o[1]

    states = [st_f[...], st_b[...]]

    def state_step(k):
        states[0] = advance(k, k * STEP, states[0], vf_ref, kt_f, lam_f, sts_f)
        states[1] = advance(k, (nst - 1 - k) * STEP, states[1], vb_ref, kt_b, lam_b, sts_b)
        if k == nst - 1:
            st_f[...] = states[0]
            st_b[...] = states[1]

    def output_step(k):
        output(k, k * STEP, False, qf_ref, vf_ref, of_ref, b_f, c_f, qe_f, sts_f)
        output(k, (nst - 1 - k) * STEP, True, qb_ref, vb_ref, ob_ref, b_b, c_b, qe_b, sts_b)

    return nst, state_step, output_step


def _hgrn_scratch(tb):
    vm = lambda dt: pltpu.VMEM((tb, GROUP), dt)
    return ([pltpu.VMEM((HDIM, GROUP), F32)] * 2 + [vm(F32)] * 6 + [vm(BF16)] * 4
            + [pltpu.VMEM((tb // STEP * HDIM, GROUP), BF16)] * 2
            + [pltpu.VMEM((GROUP, GROUP), BF16)])


def _mla_qkv(ud, ct_ref, sg_ref, qag_ref, kvag_ref, wq_ref, wk_ref,
             gq_ref, gqs_ref, gk_ref, gks_ref, q_ref, k_ref, v_ref):
    nw = HEADS * HPAD
    nq = _rms(ud[:, :GROUP], qag_ref[...]).astype(BF16)
    qq = _dot(nq, wq_ref[...])
    nkv = _rms(ud[:, GROUP:GROUP + HPAD], kvag_ref[...]).astype(BF16)
    kr = ud[:, GROUP + HPAD:]
    kr_hi = kr.astype(BF16)
    kr_lo = (kr - kr_hi.astype(F32)).astype(BF16)
    kk = _dot(jnp.concatenate([nkv, kr_hi, kr_lo], axis=-1), wk_ref[...])
    ct = ct_ref[...]
    sg = sg_ref[...]
    scale = QK ** -0.5 * LOG2E
    for h in range(HEADS):
        sl = slice(h * HPAD, (h + 1) * HPAD)
        sw = slice(nw + h * HPAD, nw + (h + 1) * HPAD)
        x = qq[:, sl]
        r = lax.rsqrt(jnp.sum(x * x, axis=-1, keepdims=True) * (1.0 / QK) + EPS) * scale
        q_ref[:, sl] = (r * (x * (gq_ref[...] * ct) + qq[:, sw] * (gqs_ref[...] * sg))).astype(BF16)
        x = kk[:, sl]
        r = lax.rsqrt(jnp.sum(x * x, axis=-1, keepdims=True) * (1.0 / QK) + EPS)
        k_ref[:, sl] = (r * (x * (gk_ref[...] * ct) + kk[:, sw] * (gks_ref[...] * sg))).astype(BF16)
    first = lax.broadcasted_iota(jnp.int32, (ud.shape[0], HPAD), 1) < HDIM
    for pair in range(HEADS // 2):
        vv = kk[:, 2 * nw + pair * HPAD:2 * nw + (pair + 1) * HPAD]
        v_ref[2 * pair] = jnp.where(first, vv, 1.0).astype(BF16)
        v_ref[2 * pair + 1] = jnp.where(first, 1.0, vv).astype(BF16)


def _attn_parts(q_ref, k_ref, v_ref, gb_ref, o_ref, sb_scr, rows, ck):
    tq = q_ref.shape[1]
    n_keys = k_ref.shape[1]
    first = lax.broadcasted_iota(jnp.int32, (rows, HPAD), 1) < HDIM
    items = [(r0, h) for r0 in range(0, tq, rows) for h in range(HEADS)]
    done = []

    def scores(n):
        r0, h = items[n]
        sl = slice(h * HPAD, (h + 1) * HPAD)
        q = q_ref[0, r0:r0 + rows, sl]
        top = None
        for c in range(0, n_keys, ck):
            sb = _dot_nt(q, k_ref[0, c:c + ck, sl]).astype(BF16)
            sb_scr[n % 2, :, c:c + ck] = sb
            for j in range(0, ck, HPAD):
                top = sb[:, j:j + HPAD] if top is None else jnp.maximum(top, sb[:, j:j + HPAD])
        return jnp.max(top, axis=-1, keepdims=True)

    def finish(n, m):
        r0, h = items[n]
        w = None
        for c in range(0, n_keys, ck):
            p = jnp.exp2(sb_scr[n % 2, :, c:c + ck] - m)
            part = _dot(p, v_ref[h, 0, c:c + ck, :])
            w = part if w is None else w + part
        w = w / pltpu.roll(w, HDIM, axis=1)
        done.append(w)
        if h == HEADS - 1:
            pairs = [jnp.where(first, done[2 * j], done[2 * j + 1]) for j in range(HEADS // 2)]
            del done[:]
            o_ref[0, r0:r0 + rows, :] = _rms(jnp.concatenate(pairs, axis=1), gb_ref[...]).astype(BF16)

    return items, scores, finish


N_HGRN_SCRATCH = 15


def _mixers_kernel(*refs, layer, rows, key_chunk):
    (gam_ref, qf_ref, zf_ref, vf_ref, qb_ref, zb_ref, vb_ref, q_ref, k_ref, v_ref, gbd_ref,
     of_ref, ob_ref, yd_ref) = refs[:14]
    hgrn_scr = refs[14:14 + N_HGRN_SCRATCH]
    sb_scr, = refs[14 + N_HGRN_SCRATCH:]

    @pl.when(pl.program_id(1) == 0)
    def _():
        for st in hgrn_scr[:2]:
            st[...] = jnp.zeros_like(st)

    nst, state_step, output_step = _hgrn_parts(gam_ref, qf_ref, zf_ref, vf_ref, qb_ref, zb_ref, vb_ref,
                                               of_ref, ob_ref, *hgrn_scr, layer)
    items, scores, finish = _attn_parts(q_ref, k_ref, v_ref, gbd_ref, yd_ref, sb_scr, rows, key_chunk)

    def share(seq, n, first, count):
        n -= first
        return seq[n * len(seq) // count:(n + 1) * len(seq) // count] if 0 <= n < count else []

    state_items = len(items) // 2
    first_out = len(items) // 4
    m_next = scores(0)
    for n in range(len(items)):
        m = m_next
        if n + 1 < len(items):
            m_next = scores(n + 1)
        finish(n, m)
        for k in share(range(nst), n, 0, state_items):
            state_step(k)
        for k in share(range(nst), n, first_out, len(items) - first_out):
            output_step(k)


def _mixers(ub, q, k, v, gamma, gbd, layer, tile):
    b, s, _ = ub.shape
    nb = s // tile
    col = lambda j, rev: pl.BlockSpec(
        (1, tile, GROUP), (lambda bi, i: (bi, nb - 1 - i, j)) if rev else (lambda bi, i: (bi, i, j)))
    main = lambda w: pl.BlockSpec((1, tile, w), lambda bi, i: (bi, i, 0))
    nw = q.shape[2]
    return pl.pallas_call(
        functools.partial(_mixers_kernel, layer=layer, rows=SUB_ROWS, key_chunk=KEY_CHUNK),
        grid=(b, nb),
        in_specs=[_full(gamma.shape),
                  col(0, False), col(1, False), col(3, False), col(0, True), col(2, True), col(3, True),
                  main(nw), pl.BlockSpec((1, s, nw), lambda bi, i: (bi, 0, 0)),
                  pl.BlockSpec((HEADS, 1, s, HPAD), lambda bi, i: (0, bi, 0, 0)),
                  _full(gbd.shape)],
        out_specs=[main(GROUP), pl.BlockSpec((1, tile, GROUP), lambda bi, i: (bi, nb - 1 - i, 0)), main(GROUP)],
        out_shape=[jax.ShapeDtypeStruct((b, s, GROUP), F32)] * 2 + [jax.ShapeDtypeStruct((b, s, GROUP), BF16)],
        scratch_shapes=_hgrn_scratch(tile) + [pltpu.VMEM((2, SUB_ROWS, s), BF16)],
        compiler_params=_params("parallel", "arbitrary"),
        name="mixers",
    )(gamma, ub, ub, ub, ub, ub, ub, q, k, v, gbd)


def _mix_out_kernel(x_ref, ya_ref, of_ref, ob_ref, g_ref, yc_ref, yd_ref, on_ref, gbb_ref, wo_ref,
                    gxq_ref, wq_ref, qn_ref, kx_ref, vx_ref, xwo_ref, o_ref, *, rows):
    tm = x_ref.shape[1]
    ones_bd = jnp.where(_head_ones(GROUP, HDIM), 1.0, 0.0).astype(BF16)
    lane = lax.broadcasted_iota(jnp.int32, (rows, GROUP), 1) // HDIM
    for r0 in range(0, tm, rows):
        sl = slice(r0, r0 + rows)
        o = of_ref[0, sl, :] + ob_ref[0, sl, :]
        g = g_ref[0, sl, :]
        yb = o * _head_rsqrt(o, ones_bd, HDIM) * on_ref[...] * (g * _sigmoid(g))
        yb = _rms(yb, gbb_ref[...]).astype(BF16)
        x1 = x_ref[0, sl, :]
        for j, y in enumerate((ya_ref[0, sl, :], yb, yc_ref[0, sl, :], yd_ref[0, sl, :])):
            x1 = x1 + _dot(y, wo_ref[0, j * GROUP:(j + 1) * GROUP, :])
        q = _dot(_rms(x1, gxq_ref[...]).astype(BF16), wq_ref[0])
        qn = q * _head_rsqrt(q, ones_bd, HDIM) * (qn_ref[...] * HDIM ** -0.5)
        acc = jnp.zeros((rows, GROUP), F32)
        for h in range(HEADS):
            s = _dot_nt(jnp.where(lane == h, qn, 0.0).astype(BF16), kx_ref[0, 0])
            p = jnp.exp(s - jnp.max(s, axis=-1, keepdims=True))
            l = jnp.sum(p, axis=-1, keepdims=True)
            pv = _dot(p.astype(BF16), vx_ref[0, 0])
            acc = jnp.where(lane == h, pv / l, acc)
        o_ref[0, sl, :] = x1 + _dot(acc.astype(BF16), xwo_ref[0])


def _ffn_kernel(x_ref, gf_ref, w13_ref, w2_ref, o_ref):
    x = x_ref[...]
    dff = w2_ref.shape[1]
    a = _dot(_rms(x, gf_ref[...]).astype(BF16), w13_ref[0])
    a1 = a[:, :dff]
    o_ref[...] = x + _dot((a1 * _sigmoid(a1) * a[:, dff:]).astype(BF16), w2_ref[0])


def _channel(x, ya, of, ob, ub, yc, yd, on, gbb, wo, gxq, wq, qn, kx, vx, xwo, gf, w13, w2, layer, tm, rows):
    b, s, d = x.shape
    n = kx.shape[2]
    row = lambda w, j=0: pl.BlockSpec((1, tm, w), lambda bi, i: (bi, i, j))
    of_layer = lambda a, nax: pl.BlockSpec((1,) + a.shape[1:], lambda *_: (layer,) + (0,) * (a.ndim - 1),
                                           pipeline_mode=pl.Buffered(1))
    x1 = pl.pallas_call(
        functools.partial(_mix_out_kernel, rows=rows),
        grid=(b, s // tm),
        in_specs=[row(d), row(GROUP), row(GROUP), row(GROUP), row(GROUP, 4), row(GROUP), row(GROUP),
                  _full(on.shape), _full(gbb.shape), of_layer(wo, 2), _full(gxq.shape), of_layer(wq, 2),
                  _full(qn.shape),
                  pl.BlockSpec((1, 1, n, GROUP), lambda bi, i: (layer, bi, 0, 0)),
                  pl.BlockSpec((1, 1, n, GROUP), lambda bi, i: (layer, bi, 0, 0)),
                  of_layer(xwo, 2)],
        out_specs=row(d),
        out_shape=jax.ShapeDtypeStruct((b, s, d), F32),
        compiler_params=_params("parallel", "parallel"),
        name="mix_out",
    )(x, ya, of, ob, ub, yc, yd, on, gbb, wo, gxq, wq, qn, kx, vx, xwo)
    t = b * s
    out = pl.pallas_call(
        _ffn_kernel,
        grid=(t // rows,),
        in_specs=[pl.BlockSpec((rows, d), lambda i: (i, 0)), _full(gf.shape), of_layer(w13, 1), of_layer(w2, 1)],
        out_specs=pl.BlockSpec((rows, d), lambda i: (i, 0)),
        out_shape=jax.ShapeDtypeStruct((t, d), F32),
        compiler_params=_params("parallel"),
        name="ffn",
    )(x1.reshape(t, d), gf, w13, w2)
    return out.reshape(b, s, d)


def _pad_heads(w, width):
    lead = w.shape[:-1]
    w = w.reshape(lead + (HEADS, width))
    w = jnp.pad(w, [(0, 0)] * len(lead) + [(0, 0), (0, HPAD - width)])
    return w.reshape(lead + (HEADS * HPAD,))


def _swap_rope(w):
    half = ROPE // 2
    z = jnp.zeros_like(w[..., :HDIM])
    return jnp.concatenate([z, w[..., HDIM + half:], w[..., HDIM:HDIM + half]], axis=-1)


def _mla_weights(wuq, wukv, qn_g, kn_g):
    dq = wuq.shape[0]
    dkv = wukv.shape[0]
    wq3 = wuq.reshape(dq, HEADS, QK)
    wq = jnp.concatenate([_pad_heads(wuq, QK), _pad_heads(_swap_rope(wq3).reshape(dq, -1), QK)], axis=1)
    kv3 = wukv.reshape(dkv, HEADS, 2 * HDIM)
    k_nope = jnp.pad(kv3[..., :HDIM], [(0, 0), (0, 0), (0, HPAD - HDIM)]).reshape(dkv, -1)
    v_cols = kv3[..., HDIM:].reshape(dkv, -1)
    top = jnp.concatenate([k_nope, jnp.zeros_like(k_nope), v_cols], axis=1)
    eye = jnp.eye(ROPE, dtype=F32)
    place = jnp.pad(eye, [(0, 0), (HDIM, HPAD - QK)])
    place_sw = jnp.pad(jnp.roll(eye, ROPE // 2, axis=1), [(0, 0), (HDIM, HPAD - QK)])
    rope_rows = jnp.concatenate([jnp.tile(place, (1, HEADS)), jnp.tile(place_sw, (1, HEADS)),
                                 jnp.zeros((ROPE, v_cols.shape[1]), F32)], axis=1)
    rope_rows = jnp.pad(rope_rows, [(0, HPAD - ROPE), (0, 0)])
    wk = jnp.concatenate([top, rope_rows, rope_rows], axis=0)
    pad1 = lambda g: jnp.pad(g, (0, HPAD - QK))[None, :]
    return (wq.astype(BF16), wk.astype(BF16),
            pad1(qn_g), pad1(_swap_rope(qn_g)), pad1(kn_g), pad1(_swap_rope(kn_g)))


def kernel(x, mem, positions, g_mix, w_in, a_dw_w, a_dw_b, a_ln_g, a_ln_b, h_gamma, h_onorm_g, c_dw_w, c_dw_b, m_qa_g, m_wuq, m_kva_g, m_wukv, m_qn_g, m_kn_g, g_branch, w_out, g_xq, g_mem, x_wq, x_wkv, x_qn_g, x_kn_g, x_wo, g_ffn, f_w13, f_w2):
    b, s, d = x.shape
    t = b * s
    depth = w_in.shape[0]
    row = lambda v: v[None, :]
    tile4 = lambda v: jnp.tile(v, HEADS)[None, :]

    inv = ROPE_BASE ** (-jnp.arange(0, ROPE, 2, dtype=F32) / ROPE)
    inv_row = jnp.concatenate([jnp.zeros((HDIM,), F32), inv, inv, jnp.zeros((HPAD - QK,), F32)])[None, :]
    ct, sg = _rope_tables(positions.astype(F32).reshape(t, 1), inv_row, TOKEN_TILE)

    kv4 = x_wkv.reshape(depth, d, HEADS, 2, HDIM).transpose(0, 1, 3, 2, 4).reshape(depth, d, 2 * GROUP)
    kx, vx = _mem_kv(mem, g_mem[:, None, :], kv4.astype(BF16), jnp.tile(x_kn_g, (1, HEADS))[:, None, :])

    wo_all, xwq_all, xwo_all = w_out.astype(BF16), x_wq.astype(BF16), x_wo.astype(BF16)
    w13_all, w2_all = f_w13.astype(BF16), f_w2.astype(BF16)
    widths = (2 * GROUP, 5 * GROUP, 3 * GROUP, 2 * GROUP)
    w_in_all = jnp.pad(w_in, [(0, 0), (0, 0), (0, sum(widths) - w_in.shape[2])]).astype(BF16)
    for l in range(depth):
        mla = (row(m_qa_g[l]), row(m_kva_g[l])) + _mla_weights(m_wuq[l], m_wukv[l], m_qn_g[l], m_kn_g[l])
        gb = g_branch[l]
        conv = (a_dw_w[l], row(a_dw_b[l]), row(a_ln_g[l]), row(a_ln_b[l]), c_dw_w[l], row(c_dw_b[l]),
                row(gb[:GROUP]), row(gb[2 * GROUP:3 * GROUP]))
        ub, qr, kr, vr, ya, yc = _in_proj(x.reshape(t, d), row(g_mix[l]), w_in_all, ct, sg, mla, conv,
                                          widths, l, TOKEN_TILE, s)
        ub = ub.reshape(b, s, -1)
        of, ob, yd = _mixers(ub, qr.reshape(b, s, -1), kr.reshape(b, s, -1), vr.reshape(HEADS, b, s, HPAD),
                             h_gamma, row(gb[3 * GROUP:]), l, TOKEN_TILE)
        ya = ya.reshape(b, s, -1)
        yc = yc.reshape(b, s, -1)
        x = _channel(x, ya, of, ob, ub, yc, yd, row(h_onorm_g[l]), row(gb[GROUP:2 * GROUP]), wo_all,
                     row(g_xq[l]), xwq_all, tile4(x_qn_g[l]), kx, vx, xwo_all, row(g_ffn[l]), w13_all, w2_all,
                     l, TOKEN_TILE, SUB_ROWS)
    return x
```
